```python
import math
import jax, jax.numpy as jnp
from jax import lax
import numpy as np

D_MODEL = 1024
BATCH = 4
SEQ = 8192
DEPTH = 2
DEC_BATCH = 128
DEC_SEQ = 8
PAST_LEN = 16384
PAGE_SIZE = 128

N_BRANCH = 4
BRANCH_W = D_MODEL // 4
GLA_HEADS = 4
GLA_DV = BRANCH_W // GLA_HEADS
GLA_DK = GLA_DV // 2
GLA_LR = 16
GLA_TAU = 16.0
GLA_CHUNK = 64
MLA_HEADS = 4
MLA_Q_LORA = D_MODEL // 4
MLA_KV_LORA = D_MODEL // 8
MLA_NOPE = 64
MLA_ROPE = 32
MLA_V = BRANCH_W // MLA_HEADS
MLA_SCALE = (MLA_NOPE + MLA_ROPE) ** -0.5
ROPE_BASE = 10000.0
FOX_HEADS = 4
FOX_DH = BRANCH_W // FOX_HEADS
FOX_SCALE = FOX_DH ** -0.5
GDN_HEADS = 4
GDN_DH = BRANCH_W // GDN_HEADS
GDN_W = GDN_HEADS * GDN_DH
GDN_CONV = 4
GDN_CHUNK = 64
PEER_HEADS = 8
PEER_NKEYS = 128
PEER_EXPERTS = PEER_NKEYS * PEER_NKEYS
PEER_QDIM = 128
PEER_HALF = PEER_QDIM // 2
PEER_TOPK = 16
PEER_BLOCK = 128
ATTN_BLOCK = 128
NORM_EPS = 1e-6

IN_SPLITS = (
    ('gla_q', GLA_HEADS * GLA_DK), ('gla_k', GLA_HEADS * GLA_DK), ('gla_v', GLA_HEADS * GLA_DV),
    ('gla_r', GLA_HEADS * GLA_DV), ('gla_a', GLA_LR),
    ('mla_cq', MLA_Q_LORA), ('mla_ckv', MLA_KV_LORA), ('mla_kr', MLA_ROPE),
    ('fox_q', FOX_HEADS * FOX_DH), ('fox_k', FOX_HEADS * FOX_DH), ('fox_v', FOX_HEADS * FOX_DH), ('fox_f', FOX_HEADS),
    ('gdn_qkv', 3 * GDN_W), ('gdn_a', GDN_HEADS), ('gdn_b', GDN_HEADS), ('gdn_g', GDN_W),
    ('gate', N_BRANCH * D_MODEL),
)
IN_WIDTH = sum(width for _, width in IN_SPLITS)

kernel_name = 'hybrid_gla_mla_fox_gdn_peer_step'


def split_in(z):
    parts, off = {}, 0
    for name, width in IN_SPLITS:
        parts[name] = z[..., off:off + width]
        off += width
    return parts


def rms_norm(x, g):
    xf = x.astype(jnp.float32)
    y = xf * lax.rsqrt(jnp.mean(xf * xf, axis=-1, keepdims=True) + NORM_EPS)
    return (y * g.astype(jnp.float32)).astype(x.dtype)


def l2_normalize(x):
    xf = x.astype(jnp.float32)
    return xf * lax.rsqrt(jnp.sum(xf * xf, axis=-1, keepdims=True) + NORM_EPS)


def apply_rope(x, pos):
    half = MLA_ROPE // 2
    inv = ROPE_BASE ** (-jnp.arange(half, dtype=jnp.float32) / half)
    ang = pos.astype(jnp.float32)[:, None] * inv[None, :]
    cos, sin = jnp.cos(ang)[:, None, :], jnp.sin(ang)[:, None, :]
    xf = x.astype(jnp.float32)
    x1, x2 = xf[..., :half], xf[..., half:]
    return jnp.concatenate([x1 * cos - x2 * sin, x2 * cos + x1 * sin], axis=-1).astype(x.dtype)


def pad_time(x, length):
    return jnp.pad(x, [(0, 0), (0, length - x.shape[1])] + [(0, 0)] * (x.ndim - 2))


def to_chunks(x, c):
    b, t = x.shape[:2]
    return jnp.moveaxis(x.reshape(b, t // c, c, *x.shape[2:]), 1, 0)


def gather_pages(pool, page_table):
    rows = pool[page_table]
    return rows.reshape(rows.shape[0], rows.shape[1] * rows.shape[2], *rows.shape[3:])


def softmax_past_new(s_past, s_new):
    t = s_new.shape[-1]
    causal = jnp.tril(jnp.ones((t, t), dtype=bool))
    s = jnp.concatenate([s_past, jnp.where(causal, s_new, -jnp.inf)], axis=-1)
    p = jax.nn.softmax(s, axis=-1)
    n_past = s_past.shape[-1]
    return p[..., :n_past], p[..., n_past:]


def chunked_gla(q, k, v, log_a, state0):
    b, t = q.shape[:2]
    c = min(GLA_CHUNK, t)
    tp = -(-t // c) * c
    qc, kc, vc, ac = [to_chunks(pad_time(a, tp), c) for a in (q, k, v, log_a)]
    mask = jnp.tril(jnp.ones((c, c), dtype=bool))

    def step(s, inp):
        qb, kb, vb, ab = inp
        cum = jnp.cumsum(ab, axis=1)
        diff = cum[:, :, None] - cum[:, None, :]
        decay = jnp.exp(jnp.where(mask[None, :, :, None, None], diff, -jnp.inf))
        attn = jnp.einsum('bthd,bshd,btshd->bhts', qb, kb, decay)
        o = jnp.einsum('bhts,bshv->bthv', attn, vb) + jnp.einsum('bthd,bhdv->bthv', qb * jnp.exp(cum), s)
        last = cum[:, -1]
        s = s * jnp.exp(last)[..., None] + jnp.einsum('bshd,bshv->bhdv', kb * jnp.exp(last[:, None] - cum), vb)
        return s, o

    s, o = lax.scan(step, state0.astype(jnp.float32), (qc, kc, vc, ac))
    o = jnp.moveaxis(o, 0, 1).reshape(b, tp, GLA_HEADS, GLA_DV)[:, :t]
    return o, s


def gla_mixer(q, k, v, r, a_lr, w_a2, b_a, norm_g, state0):
    b, t, _ = q.shape
    qh = q.reshape(b, t, GLA_HEADS, GLA_DK).astype(jnp.float32) * (GLA_DK ** -0.5)
    kh = k.reshape(b, t, GLA_HEADS, GLA_DK).astype(jnp.float32)
    vh = v.reshape(b, t, GLA_HEADS, GLA_DV).astype(jnp.float32)
    log_a = jax.nn.log_sigmoid((a_lr @ w_a2 + b_a).astype(jnp.float32)) / GLA_TAU
    log_a = log_a.reshape(b, t, GLA_HEADS, GLA_DK)
    o, s = chunked_gla(qh, kh, vh, log_a, state0)
    o = rms_norm(o.astype(q.dtype), norm_g) * jax.nn.silu(r.reshape(b, t, GLA_HEADS, GLA_DV))
    return o.reshape(b, t, BRANCH_W), s


def mla_project(cq, ckv, kr, pos, qnorm_g, w_uq, kvnorm_g, w_uk):
    q = jnp.einsum('btr,rhe->bthe', rms_norm(cq, qnorm_g), w_uq)
    q_nope = q[..., :MLA_NOPE]
    q_rope = apply_rope(q[..., MLA_NOPE:], pos)
    q_lat = jnp.einsum('bthn,chn->bthc', q_nope, w_uk)
    c = rms_norm(ckv, kvnorm_g)
    k_rope = apply_rope(kr[:, :, None, :], pos)[:, :, 0, :]
    return q_lat, q_rope, c, k_rope


def mla_prompt(q_lat, q_rope, c, kr):
    b, s = q_lat.shape[:2]
    nb = s // ATTN_BLOCK
    blocks = lambda a: jnp.swapaxes(a.reshape(b, nb, ATTN_BLOCK, *a.shape[2:]), 0, 1)
    qpos = jnp.arange(s).reshape(nb, ATTN_BLOCK)
    kpos = jnp.arange(s)

    def one_block(inp):
        ql, qr, qp = inp
        sc = (jnp.einsum('bqhc,bkc->bhqk', ql, c) + jnp.einsum('bqhr,bkr->bhqk', qr, kr)).astype(jnp.float32) * MLA_SCALE
        sc = jnp.where(qp[:, None] >= kpos[None, :], sc, -jnp.inf)
        pr = jax.nn.softmax(sc, axis=-1).astype(c.dtype)
        return jnp.einsum('bhqk,bkc->bqhc', pr, c)

    o = lax.map(one_block, (blocks(q_lat), blocks(q_rope), qpos))
    return jnp.swapaxes(o, 0, 1).reshape(b, s, MLA_HEADS, MLA_KV_LORA)


def mla_sample(q_lat, q_rope, c_new, r_new, c_past, r_past):
    s_past = (jnp.einsum('bqhc,bkc->bhqk', q_lat, c_past) + jnp.einsum('bqhr,bkr->bhqk', q_rope, r_past)).astype(jnp.float32) * MLA_SCALE
    s_new = (jnp.einsum('bqhc,bkc->bhqk', q_lat, c_new) + jnp.einsum('bqhr,bkr->bhqk', q_rope, r_new)).astype(jnp.float32) * MLA_SCALE
    p_past, p_new = softmax_past_new(s_past, s_new)
    return (jnp.einsum('bhqk,bkc->bqhc', p_past.astype(c_past.dtype), c_past)
            + jnp.einsum('bhqk,bkc->bqhc', p_new.astype(c_new.dtype), c_new))


def fox_prompt(q, k, v, logf):
    b, s = q.shape[:2]
    nb = s // ATTN_BLOCK
    c = jnp.cumsum(logf, axis=1)
    c_keys = jnp.swapaxes(c, 1, 2)
    blocks = lambda a: jnp.swapaxes(a.reshape(b, nb, ATTN_BLOCK, *a.shape[2:]), 0, 1)
    qpos = jnp.arange(s).reshape(nb, ATTN_BLOCK)
    kpos = jnp.arange(s)

    def one_block(inp):
        qb, cb, qp = inp
        sc = jnp.einsum('bqhd,bkhd->bhqk', qb, k).astype(jnp.float32) * FOX_SCALE
        sc = sc + jnp.swapaxes(cb, 1, 2)[..., :, None] - c_keys[:, :, None, :]
        sc = jnp.where(qp[:, None] >= kpos[None, :], sc, -jnp.inf)
        pr = jax.nn.softmax(sc, axis=-1).astype(v.dtype)
        return jnp.einsum('bhqk,bkhd->bqhd', pr, v)

    o = lax.map(one_block, (blocks(q), blocks(c), qpos))
    return jnp.swapaxes(o, 0, 1).reshape(b, s, FOX_HEADS, FOX_DH)


def fox_sample(q, k, v, logf, k_past, v_past, logf_past):
    lp = logf_past.astype(jnp.float32)
    after = lax.cumsum(lp, axis=1, reverse=True) - lp
    cn = jnp.cumsum(logf, axis=1)
    cn_t = jnp.swapaxes(cn, 1, 2)
    after_t = jnp.swapaxes(after, 1, 2)
    s_past = jnp.einsum('bqhd,bkhd->bhqk', q, k_past).astype(jnp.float32) * FOX_SCALE + cn_t[..., :, None] + after_t[:, :, None, :]
    s_new = jnp.einsum('bqhd,bkhd->bhqk', q, k).astype(jnp.float32) * FOX_SCALE + cn_t[..., :, None] - cn_t[..., None, :]
    p_past, p_new = softmax_past_new(s_past, s_new)
    return (jnp.einsum('bhqk,bkhd->bqhd', p_past.astype(v_past.dtype), v_past)
            + jnp.einsum('bhqk,bkhd->bqhd', p_new.astype(v.dtype), v))


def causal_conv(x, buf, w):
    t = x.shape[1]
    xx = jnp.concatenate([buf.astype(x.dtype), x], axis=1)
    y = sum(xx[:, i:i + t] * w[i] for i in range(GDN_CONV))
    return y, xx[:, xx.shape[1] - (GDN_CONV - 1):]


def chunked_gated_delta(q, k, v, log_a, beta, state0):
    b, t = q.shape[:2]
    c = min(GDN_CHUNK, t)
    tp = -(-t // c) * c
    qc, kc, vc, ac, bc = [to_chunks(pad_time(a.astype(jnp.float32), tp), c) for a in (q, k, v, log_a, beta)]
    incl = jnp.tril(jnp.ones((c, c), dtype=bool))
    strict = jnp.tril(jnp.ones((c, c), dtype=bool), -1)
    eye = jnp.eye(c, dtype=jnp.float32)

    def step(s, inp):
        qb, kb, vb, ab, bb = [jnp.swapaxes(a, 1, 2) for a in inp]
        g = jnp.cumsum(ab, axis=-1)
        lmat = jnp.exp(jnp.where(incl, g[..., :, None] - g[..., None, :], -jnp.inf))
        kbeta = kb * bb[..., None]
        m = jnp.where(strict, jnp.einsum('bhid,bhjd->bhij', kbeta, kb) * lmat, 0.0)
        rhs = jnp.concatenate([vb * bb[..., None], kbeta * jnp.exp(g)[..., None]], axis=-1)
        sol = lax.linalg.triangular_solve(m + eye, rhs, left_side=True, lower=True, unit_diagonal=True)
        u, w = sol[..., :GDN_DH], sol[..., GDN_DH:]
        v_new = u - jnp.einsum('bhck,bhkv->bhcv', w, s)
        attn = jnp.einsum('bhik,bhjk->bhij', qb, kb) * lmat
        o = jnp.einsum('bhck,bhkv->bhcv', qb * jnp.exp(g)[..., None], s) + jnp.einsum('bhij,bhjv->bhiv', attn, v_new)
        g_last = g[..., -1]
        s = s * jnp.exp(g_last)[..., None, None] + jnp.einsum('bhck,bhcv->bhkv', kb * jnp.exp(g_last[..., None] - g)[..., None], v_new)
        return s, jnp.swapaxes(o, 1, 2)

    s, o = lax.scan(step, state0.astype(jnp.float32), (qc, kc, vc, ac, bc))
    o = jnp.moveaxis(o, 0, 1).reshape(b, tp, GDN_HEADS, GDN_DH)[:, :t]
    return o, s


def gdn_mixer(qkv, a, beta_logit, gate, conv_w, a_log, dt_bias, norm_g, state0, buf0):
    b, t, _ = qkv.shape
    conv, buf = causal_conv(qkv, buf0, conv_w)
    conv = jax.nn.silu(conv)
    q = l2_normalize(conv[..., :GDN_W].reshape(b, t, GDN_HEADS, GDN_DH)) * (GDN_DH ** -0.5)
    k = l2_normalize(conv[..., GDN_W:2 * GDN_W].reshape(b, t, GDN_HEADS, GDN_DH))
    v = conv[..., 2 * GDN_W:].reshape(b, t, GDN_HEADS, GDN_DH).astype(jnp.float32)
    log_alpha = -jnp.exp(a_log.astype(jnp.float32)) * jax.nn.softplus(a.astype(jnp.float32) + dt_bias.astype(jnp.float32))
    beta = jax.nn.sigmoid(beta_logit.astype(jnp.float32))
    o, s = chunked_gated_delta(q, k, v, log_alpha, beta, state0)
    o = rms_norm(o.astype(qkv.dtype), norm_g) * jax.nn.silu(gate.reshape(b, t, GDN_HEADS, GDN_DH))
    return o.reshape(b, t, GDN_W), s, buf


def peer_ffn(x, w_q, subkeys, u, v):
    b, t, d = x.shape
    n = b * t
    n_blocks = -(-n // PEER_BLOCK)
    xb = jnp.pad(x.reshape(n, d), ((0, n_blocks * PEER_BLOCK - n), (0, 0))).reshape(n_blocks, PEER_BLOCK, d)

    def one_block(xt):
        q = (xt @ w_q).reshape(PEER_BLOCK, PEER_HEADS, 2, PEER_HALF)
        sub = jnp.einsum('thpd,hpkd->thpk', q, subkeys).astype(jnp.float32)
        top_s, top_i = lax.top_k(sub, PEER_TOPK)
        cand_s = (top_s[..., 0, :, None] + top_s[..., 1, None, :]).reshape(PEER_BLOCK, PEER_HEADS, PEER_TOPK * PEER_TOPK)
        cand_i = (top_i[..., 0, :, None] * PEER_NKEYS + top_i[..., 1, None, :]).reshape(PEER_BLOCK, PEER_HEADS, PEER_TOPK * PEER_TOPK)
        best_s, best_pos = lax.top_k(cand_s, PEER_TOPK)
        idx = jnp.take_along_axis(cand_i, best_pos, axis=-1)
        gate = jax.nn.softmax(best_s, axis=-1)
        act = jax.nn.gelu(jnp.einsum('thkd,td->thk', jnp.take(u, idx, axis=0), xt).astype(jnp.float32), approximate=False)
        return jnp.einsum('thk,thkd->td', (gate * act).astype(xt.dtype), jnp.take(v, idx, axis=0))

    y = lax.map(one_block, xb)
    return y.reshape(n_blocks * PEER_BLOCK, d)[:n].reshape(b, t, d)


def token_mix(xn, p, l, pos, past):
    b, t, _ = xn.shape
    z = split_in(xn @ p['w_in'][l])
    gla_s0 = jnp.zeros((b, GLA_HEADS, GLA_DK, GLA_DV), jnp.float32) if past is None else past['state_gla'][l]
    o_a, gla_s = gla_mixer(z['gla_q'], z['gla_k'], z['gla_v'], z['gla_r'], z['gla_a'],
                           p['gla_w_a2'][l], p['gla_b_a'][l], p['gla_norm_g'][l], gla_s0)
    q_lat, q_rope, c_lat, k_rope = mla_project(z['mla_cq'], z['mla_ckv'], z['mla_kr'], pos, p['mla_qnorm_g'][l],
                                               p['mla_w_uq'][l], p['mla_kvnorm_g'][l], p['mla_w_uk'][l])
    if past is None:
        o_lat = mla_prompt(q_lat, q_rope, c_lat, k_rope)
    else:
        o_lat = mla_sample(q_lat, q_rope, c_lat, k_rope,
                           gather_pages(past['cache_mla_latent'][l], past['page_table']),
                           gather_pages(past['cache_mla_rope'][l], past['page_table']))
    o_b = jnp.einsum('bthc,chv->bthv', o_lat, p['mla_w_uv'][l]).reshape(b, t, BRANCH_W)
    fq = z['fox_q'].reshape(b, t, FOX_HEADS, FOX_DH)
    fk = z['fox_k'].reshape(b, t, FOX_HEADS, FOX_DH)
    fv = z['fox_v'].reshape(b, t, FOX_HEADS, FOX_DH)
    logf = jax.nn.log_sigmoid((z['fox_f'] + p['fox_b_f'][l]).astype(jnp.float32))
    if past is None:
        o_c = fox_prompt(fq, fk, fv, logf)
    else:
        o_c = fox_sample(fq, fk, fv, logf,
                         gather_pages(past['cache_fox_k'][l], past['page_table']),
                         gather_pages(past['cache_fox_v'][l], past['page_table']),
                         gather_pages(past['cache_fox_logf'][l], past['page_table']))
    o_c = o_c.reshape(b, t, BRANCH_W)
    if past is None:
        gdn_s0 = jnp.zeros((b, GDN_HEADS, GDN_DH, GDN_DH), jnp.float32)
        gdn_buf0 = jnp.zeros((b, GDN_CONV - 1, 3 * GDN_W), xn.dtype)
    else:
        gdn_s0 = past['state_gdn'][l]
        gdn_buf0 = past['state_gdn_conv'][l]
    o_d, gdn_s, gdn_buf = gdn_mixer(z['gdn_qkv'], z['gdn_a'], z['gdn_b'], z['gdn_g'], p['gdn_conv_w'][l],
                                    p['gdn_a_log'][l], p['gdn_dt_bias'][l], p['gdn_norm_g'][l], gdn_s0, gdn_buf0)
    gates = jax.nn.sigmoid(z['gate'].reshape(b, t, N_BRANCH, D_MODEL).astype(jnp.float32)).astype(xn.dtype)
    branches = jnp.stack([o_a, o_b.astype(xn.dtype), o_c.astype(xn.dtype), o_d], axis=2)
    merged = jnp.einsum('btnw,nwd->btnd', branches, p['w_branch'][l])
    out = jnp.sum(gates * merged, axis=2) @ p['w_out'][l]
    state = {'mla_latent': c_lat, 'mla_rope': k_rope, 'fox_k': fk, 'fox_v': fv, 'fox_logf': logf,
             'gla': gla_s, 'gdn': gdn_s, 'gdn_conv': gdn_buf}
    return out, state


def run_trunk(x, p, pos, past):
    states = []
    for l in range(DEPTH):
        mix, st = token_mix(rms_norm(x, p['norm1_g'][l]), p, l, pos, past)
        x = x + mix
        x = x + peer_ffn(rms_norm(x, p['norm2_g'][l]), p['peer_w_q'][l], p['peer_subkeys'][l], p['peer_u'][l], p['peer_v'][l])
        states.append(st)
    stacked = {name: jnp.stack([st[name] for st in states]) for name in states[0]}
    return rms_norm(x, p['final_norm_g']), stacked


def setup_inputs(seed: int = 0) -> dict:
    key = jax.random.key(seed)
    keys = iter(jax.random.split(key, 64))

    def nrm(shape, scale):
        return scale * jax.random.normal(next(keys), shape, jnp.float32)

    def gain(shape):
        return 1.0 + nrm(shape, 0.1)

    n_pages = PAST_LEN // PAGE_SIZE
    used = DEC_BATCH * n_pages
    n_pool = used + max(1, used // 4)
    page_table = jax.random.permutation(next(keys), n_pool)[:used].reshape(DEC_BATCH, n_pages).astype(jnp.int32)
    dt = jnp.exp(jax.random.uniform(next(keys), (DEPTH, GDN_HEADS), jnp.float32, math.log(1e-3), math.log(1e-1)))
    dt_bias = jnp.log(jnp.expm1(dt))
    return {
        'x_prompt': nrm((BATCH, SEQ, D_MODEL), 1.0),
        'x_sample': nrm((DEC_BATCH, DEC_SEQ, D_MODEL), 1.0),
        'cache_mla_latent': nrm((DEPTH, n_pool, PAGE_SIZE, MLA_KV_LORA), 1.0),
        'cache_mla_rope': nrm((DEPTH, n_pool, PAGE_SIZE, MLA_ROPE), 1.0),
        'cache_fox_k': nrm((DEPTH, n_pool, PAGE_SIZE, FOX_HEADS, FOX_DH), 1.0),
        'cache_fox_v': nrm((DEPTH, n_pool, PAGE_SIZE, FOX_HEADS, FOX_DH), 1.0),
        'cache_fox_logf': jax.nn.log_sigmoid(2.0 + nrm((DEPTH, n_pool, PAGE_SIZE, FOX_HEADS), 0.5)),
        'state_gla': nrm((DEPTH, DEC_BATCH, GLA_HEADS, GLA_DK, GLA_DV), 0.5),
        'state_gdn': nrm((DEPTH, DEC_BATCH, GDN_HEADS, GDN_DH, GDN_DH), GDN_DH ** -0.5),
        'state_gdn_conv': nrm((DEPTH, DEC_BATCH, GDN_CONV - 1, 3 * GDN_W), 1.0),
        'page_table': page_table,
        'norm1_g': gain((DEPTH, D_MODEL)),
        'w_in': nrm((DEPTH, D_MODEL, IN_WIDTH), D_MODEL ** -0.5),
        'gla_w_a2': nrm((DEPTH, GLA_LR, GLA_HEADS * GLA_DK), GLA_LR ** -0.5),
        'gla_b_a': nrm((DEPTH, GLA_HEADS * GLA_DK), 0.1),
        'gla_norm_g': gain((DEPTH, GLA_DV)),
        'mla_qnorm_g': gain((DEPTH, MLA_Q_LORA)),
        'mla_w_uq': nrm((DEPTH, MLA_Q_LORA, MLA_HEADS, MLA_NOPE + MLA_ROPE), MLA_Q_LORA ** -0.5),
        'mla_kvnorm_g': gain((DEPTH, MLA_KV_LORA)),
        'mla_w_uk': nrm((DEPTH, MLA_KV_LORA, MLA_HEADS, MLA_NOPE), MLA_KV_LORA ** -0.5),
        'mla_w_uv': nrm((DEPTH, MLA_KV_LORA, MLA_HEADS, MLA_V), MLA_KV_LORA ** -0.5),
        'fox_b_f': 2.0 + nrm((DEPTH, FOX_HEADS), 0.1),
        'gdn_conv_w': nrm((DEPTH, GDN_CONV, 3 * GDN_W), GDN_CONV ** -0.5),
        'gdn_a_log': jnp.log(jax.random.uniform(next(keys), (DEPTH, GDN_HEADS), jnp.float32, 1.0, 16.0)),
        'gdn_dt_bias': dt_bias,
        'gdn_norm_g': gain((DEPTH, GDN_DH)),
        'w_branch': nrm((DEPTH, N_BRANCH, BRANCH_W, D_MODEL), BRANCH_W ** -0.5),
        'w_out': nrm((DEPTH, D_MODEL, D_MODEL), 0.5 * D_MODEL ** -0.5),
        'norm2_g': gain((DEPTH, D_MODEL)),
        'peer_w_q': nrm((DEPTH, D_MODEL, PEER_HEADS * PEER_QDIM), D_MODEL ** -0.5),
        'peer_subkeys': nrm((DEPTH, PEER_HEADS, 2, PEER_NKEYS, PEER_HALF), PEER_HALF ** -0.5),
        'peer_u': nrm((DEPTH, PEER_EXPERTS, D_MODEL), D_MODEL ** -0.5),
        'peer_v': nrm((DEPTH, PEER_EXPERTS, D_MODEL), 0.5 * PEER_HEADS ** -0.5),
        'final_norm_g': gain((D_MODEL,)),
    }


def reference(x_prompt, x_sample, cache_mla_latent, cache_mla_rope, cache_fox_k, cache_fox_v, cache_fox_logf,
              state_gla, state_gdn, state_gdn_conv, page_table,
              norm1_g, w_in, gla_w_a2, gla_b_a, gla_norm_g, mla_qnorm_g, mla_w_uq, mla_kvnorm_g, mla_w_uk, mla_w_uv,
              fox_b_f, gdn_conv_w, gdn_a_log, gdn_dt_bias, gdn_norm_g, w_branch, w_out, norm2_g,
              peer_w_q, peer_subkeys, peer_u, peer_v, final_norm_g):
    params = {
        'norm1_g': norm1_g, 'w_in': w_in, 'gla_w_a2': gla_w_a2, 'gla_b_a': gla_b_a, 'gla_norm_g': gla_norm_g,
        'mla_qnorm_g': mla_qnorm_g, 'mla_w_uq': mla_w_uq, 'mla_kvnorm_g': mla_kvnorm_g, 'mla_w_uk': mla_w_uk,
        'mla_w_uv': mla_w_uv, 'fox_b_f': fox_b_f, 'gdn_conv_w': gdn_conv_w, 'gdn_a_log': gdn_a_log,
        'gdn_dt_bias': gdn_dt_bias, 'gdn_norm_g': gdn_norm_g, 'w_branch': w_branch, 'w_out': w_out,
        'norm2_g': norm2_g, 'peer_w_q': peer_w_q, 'peer_subkeys': peer_subkeys, 'peer_u': peer_u,
        'peer_v': peer_v, 'final_norm_g': final_norm_g,
    }
    past = {
        'cache_mla_latent': cache_mla_latent, 'cache_mla_rope': cache_mla_rope, 'cache_fox_k': cache_fox_k,
        'cache_fox_v': cache_fox_v, 'cache_fox_logf': cache_fox_logf, 'state_gla': state_gla,
        'state_gdn': state_gdn, 'state_gdn_conv': state_gdn_conv, 'page_table': page_table,
    }
    past_len = page_table.shape[1] * cache_mla_latent.shape[2]
    pos_prompt = jnp.arange(x_prompt.shape[1], dtype=jnp.int32)
    pos_sample = past_len + jnp.arange(x_sample.shape[1], dtype=jnp.int32)
    y_prompt, st_p = run_trunk(x_prompt, params, pos_prompt, None)
    y_sample, st_s = run_trunk(x_sample, params, pos_sample, past)
    return (y_prompt, y_sample,
            st_p['mla_latent'], st_s['mla_latent'], st_p['mla_rope'], st_s['mla_rope'],
            st_p['fox_k'], st_s['fox_k'], st_p['fox_v'], st_s['fox_v'],
            st_p['fox_logf'], st_s['fox_logf'], st_p['gla'], st_s['gla'],
            st_p['gdn'], st_s['gdn'], st_p['gdn_conv'], st_s['gdn_conv'])
```

```python
import functools
import math

import jax
import jax.numpy as jnp
from jax import lax
from jax.experimental import pallas as pl
from jax.experimental.pallas import tpu as pltpu

F32 = jnp.float32
BF16 = jnp.bfloat16
I32 = jnp.int32
HI = lax.Precision.HIGHEST

NORM_EPS = 1e-6
N_HEADS = 4
GLA_DK, GLA_DV, GLA_LR, GLA_TAU, GLA_CHUNK = 32, 64, 16, 16.0, 64
MLA_NOPE, MLA_ROPE, MLA_KV_LORA, MLA_Q_LORA, MLA_V = 64, 32, 128, 256, 64
MLA_SCALE = (MLA_NOPE + MLA_ROPE) ** -0.5
ROPE_BASE = 10000.0
FOX_DH = 64
FOX_SCALE = FOX_DH ** -0.5
GDN_DH, GDN_CONV, GDN_CHUNK = 64, 4, 64
GDN_W = N_HEADS * GDN_DH
PEER_HEADS, PEER_NKEYS, PEER_HALF, PEER_TOPK = 8, 128, 64, 16
BRANCH_W = 256
PAGE = 128

LANES = 128
SUBLANES = 8
VMEM_LIMIT = 56 * 1024 * 1024

ROW_TILE = 512
SEQ_TILE = 512
ATTN_TILE = 512
SELECT_TILE = 256
GATHER_TOKENS = 8
DEC_PAGES = 8


def _tile(n, pref):
    t = min(n, pref)
    assert n % t == 0, (n, pref)
    return t


def _cparams(sem):
    return pltpu.CompilerParams(dimension_semantics=sem, vmem_limit_bytes=VMEM_LIMIT)


def _rms(x, g):
    return x * lax.rsqrt(jnp.mean(x * x, axis=-1, keepdims=True) + NORM_EPS) * g


def _softplus(x):
    return jnp.maximum(x, 0.0) + jnp.log1p(jnp.exp(-jnp.abs(x)))


def _log_sigmoid(x):
    return -_softplus(-x)


def _sigmoid(x):
    return 1.0 / (1.0 + jnp.exp(-x))


def _iota(shape, axis):
    return lax.broadcasted_iota(I32, shape, axis)


def _dot(a, b, precision=None):
    return jnp.dot(a, b, precision=precision, preferred_element_type=F32)


def _dot_nt(a, b, precision=None):
    return lax.dot_general(a, b, (((1,), (1,)), ((), ())), precision=precision, preferred_element_type=F32)


def _dot_tn(a, b, precision=None):
    return lax.dot_general(a, b, (((0,), (0,)), ((), ())), precision=precision, preferred_element_type=F32)


def _group_ones(n, group, dtype=F32):
    return (_iota((n, n), 0) // group == _iota((n, n), 1) // group).astype(dtype)


def _norm_proj_kernel(x_ref, g_ref, w_ref, o_ref, xn_ref):
    @pl.when(pl.program_id(1) == 0)
    def _():
        xn_ref[...] = _rms(x_ref[...], g_ref[...]).astype(BF16)

    o_ref[...] = _dot(xn_ref[...], w_ref[...])


def norm_proj(x, g, w):
    n, d = x.shape
    m = w.shape[1]
    tm = _tile(n, ROW_TILE)
    tn = m if m <= 1280 else _tile(m, 1024)
    return pl.pallas_call(
        _norm_proj_kernel,
        grid=(n // tm, m // tn),
        in_specs=[pl.BlockSpec((tm, d), lambda i, j: (i, 0)),
                  pl.BlockSpec((1, d), lambda i, j: (0, 0)),
                  pl.BlockSpec((d, tn), lambda i, j: (0, j))],
        out_specs=pl.BlockSpec((tm, tn), lambda i, j: (i, j)),
        out_shape=jax.ShapeDtypeStruct((n, m), F32),
        scratch_shapes=[pltpu.VMEM((tm, d), BF16)],
        compiler_params=_cparams(("parallel", "arbitrary")),
        name="norm_proj",
    )(x, g, w)


def _rmsnorm_kernel(x_ref, g_ref, o_ref):
    o_ref[...] = _rms(x_ref[...], g_ref[...])


def rmsnorm(x, g):
    n, d = x.shape
    tm = _tile(n, ROW_TILE)
    return pl.pallas_call(
        _rmsnorm_kernel,
        grid=(n // tm,),
        in_specs=[pl.BlockSpec((tm, d), lambda i: (i, 0)), pl.BlockSpec((1, d), lambda i: (0, 0))],
        out_specs=pl.BlockSpec((tm, d), lambda i: (i, 0)),
        out_shape=jax.ShapeDtypeStruct((n, d), F32),
        compiler_params=_cparams(("parallel",)),
        name="final_norm",
    )(x, g)


def _gla_kernel(z_ref, wa_ref, ba_ref, ng_ref, s0_ref, o_ref, sout_ref,
                st_ref, cum_ref, q_ref, k_ref, v_ref, oi_ref, *, chunk, n_chunks):
    c = chunk
    hk = N_HEADS * GLA_DK
    hv = N_HEADS * GLA_DV

    @pl.when(pl.program_id(1) == 0)
    def _():
        st_ref[...] = s0_ref[0]

    tril = (_iota((c, c), 1) <= _iota((c, c), 0)).astype(F32)
    expand = (_iota((hk, hv), 0) // GLA_DK == _iota((hk, hv), 1) // GLA_DV).astype(BF16)
    state_mask = _iota((hv, hk), 0) // GLA_DV == _iota((hv, hk), 1) // GLA_DK
    group_mean = _group_ones(hv, GLA_DV) * (1.0 / GLA_DV)
    srow = _iota((c, hk), 0)

    def one_chunk(ci, carry):
        base = pl.multiple_of(ci * c, c)
        zc = z_ref[0, pl.ds(base, c), :]
        q = zc[:, 0:hk] * (GLA_DK ** -0.5)
        k = zc[:, hk:2 * hk]
        v = zc[:, 2 * hk:2 * hk + hv]
        r = zc[:, 2 * hk + hv:2 * hk + 2 * hv]
        a = zc[:, 2 * hk + 2 * hv:2 * hk + 2 * hv + LANES]
        log_a = _log_sigmoid(_dot(a, wa_ref[...], HI) + ba_ref[...]) * (1.0 / GLA_TAU)
        cum = _dot(tril, log_a, HI)
        cum_ref[...] = cum
        q_ref[...] = q
        k_ref[...] = k
        v_ref[...] = v

        def one_row(t, carry2):
            diff = cum_ref[pl.ds(t, 1), :] - cum_ref[...]
            decay = jnp.where(srow <= t, jnp.exp(diff), 0.0)
            prod = (q_ref[pl.ds(t, 1), :] * k_ref[...]) * decay
            attn = _dot(prod.astype(BF16), expand)
            oi_ref[pl.ds(t, 1), :] = jnp.sum(attn * v_ref[...], axis=0, keepdims=True)
            return carry2

        lax.fori_loop(0, c, one_row, 0, unroll=min(c, 8))

        st = st_ref[...]
        o = oi_ref[...] + _dot_nt((q * jnp.exp(cum)).astype(BF16), st.astype(BF16))
        last = cum[c - 1:c, :]
        kd = k * jnp.exp(last - cum)
        upd = _dot_tn(v.astype(BF16), kd.astype(BF16))
        st_ref[...] = st * jnp.exp(last) + jnp.where(state_mask, upd, 0.0)

        ms = _dot(o * o, group_mean, HI)
        y = o * lax.rsqrt(ms + NORM_EPS) * ng_ref[...] * (r * _sigmoid(r))
        o_ref[0, pl.ds(base, c), :] = y
        return carry

    lax.fori_loop(0, n_chunks, one_chunk, 0)

    @pl.when(pl.program_id(1) == pl.num_programs(1) - 1)
    def _():
        sout_ref[0] = st_ref[...]


def gla_mixer(z, wa, ba, ng, s0t, chunk):
    b, t, w = z.shape
    tb = _tile(t, SEQ_TILE)
    hk, hv = N_HEADS * GLA_DK, N_HEADS * GLA_DV
    kern = functools.partial(_gla_kernel, chunk=chunk, n_chunks=tb // chunk)
    return pl.pallas_call(
        kern,
        grid=(b, t // tb),
        in_specs=[pl.BlockSpec((1, tb, w), lambda i, j: (i, j, 0)),
                  pl.BlockSpec((LANES, hk), lambda i, j: (0, 0)),
                  pl.BlockSpec((1, hk), lambda i, j: (0, 0)),
                  pl.BlockSpec((1, hv), lambda i, j: (0, 0)),
                  pl.BlockSpec((1, hv, hk), lambda i, j: (i, 0, 0))],
        out_specs=[pl.BlockSpec((1, tb, hv), lambda i, j: (i, j, 0)),
                   pl.BlockSpec((1, hv, hk), lambda i, j: (i, 0, 0))],
        out_shape=[jax.ShapeDtypeStruct((b, t, hv), F32), jax.ShapeDtypeStruct((b, hv, hk), F32)],
        scratch_shapes=[pltpu.VMEM((hv, hk), F32), pltpu.VMEM((chunk, hk), F32), pltpu.VMEM((chunk, hk), F32),
                        pltpu.VMEM((chunk, hk), F32), pltpu.VMEM((chunk, hv), F32), pltpu.VMEM((chunk, hv), F32)],
        compiler_params=_cparams(("parallel", "arbitrary")),
        name="gla_mixer",
    )(z, wa, ba, ng, s0t)


def _unit_lower_inverse(m, c):
    eye = (_iota((c, c), 0) == _iota((c, c), 1)).astype(F32)
    p = -m
    inv = eye + p
    for _ in range(int(math.log2(c)) - 1):
        p = _dot(p, p, HI)
        inv = inv + _dot(inv, p, HI)
    return inv


def _gdn_kernel(z_ref, cw_ref, buf0_ref, alog_ref, dtb_ref, ng_ref, s0_ref, o_ref, sout_ref,
                s_ref, tail_ref, *, chunk, n_chunks):
    c = chunk
    w3 = 3 * GDN_W

    @pl.when(pl.program_id(1) == 0)
    def _():
        s_ref[...] = s0_ref[0]
        tail_ref[...] = buf0_ref[0]

    tril = (_iota((c, c), 1) <= _iota((c, c), 0)).astype(F32)
    triu = (_iota((c, c), 0) <= _iota((c, c), 1)).astype(F32)
    incl = _iota((c, c), 1) <= _iota((c, c), 0)
    strict = _iota((c, c), 1) < _iota((c, c), 0)
    group_ones = _group_ones(GDN_W, GDN_DH)
    row8 = _iota((SUBLANES, w3), 0)
    a_neg = -jnp.exp(alog_ref[...])

    def one_chunk(ci, carry):
        base = pl.multiple_of(ci * c, c)
        zc = z_ref[0, pl.ds(base, c), :]
        x = zc[:, 0:w3]
        gate = zc[:, w3:w3 + GDN_W]
        ab = zc[:, w3 + GDN_W:w3 + GDN_W + LANES]
        tail = tail_ref[...]
        conv = x * cw_ref[GDN_CONV - 1:GDN_CONV, :]
        for d in range(1, GDN_CONV):
            rolled = pltpu.roll(x, d, 0)
            head = jnp.where(row8 < d, pltpu.roll(tail, d, 0), rolled[0:SUBLANES, :])
            if c > SUBLANES:
                delayed = jnp.concatenate([head, rolled[SUBLANES:, :]], axis=0)
            else:
                delayed = head
            conv = conv + delayed * cw_ref[GDN_CONV - 1 - d:GDN_CONV - d, :]
        tail_ref[...] = x[c - SUBLANES:c, :]
        cv = conv * _sigmoid(conv)
        q = cv[:, 0:GDN_W]
        k = cv[:, GDN_W:2 * GDN_W]
        v = cv[:, 2 * GDN_W:3 * GDN_W]
        q = q * lax.rsqrt(_dot(q * q, group_ones, HI) + NORM_EPS) * (GDN_DH ** -0.5)
        k = k * lax.rsqrt(_dot(k * k, group_ones, HI) + NORM_EPS)
        log_alpha = a_neg * _softplus(ab + dtb_ref[...])
        log_alpha = jnp.where(_iota((c, LANES), 1) < N_HEADS, log_alpha, 0.0)
        beta = _sigmoid(ab)
        g = _dot(tril, log_alpha, HI)
        g_t = _dot_tn(log_alpha, triu, HI)
        eg = jnp.exp(g)
        outs = []
        for h in range(N_HEADS):
            sl = slice(h * GDN_DH, (h + 1) * GDN_DH)
            gh = g[:, h:h + 1]
            lmat = jnp.where(incl, jnp.exp(gh - g_t[h:h + 1, :]), 0.0)
            bh = beta[:, N_HEADS + h:N_HEADS + h + 1]
            qh, kh, vh = q[:, sl], k[:, sl], v[:, sl]
            kb = kh * bh
            m = jnp.where(strict, _dot_nt(kb, kh, HI) * lmat, 0.0)
            inv = _unit_lower_inverse(m, c)
            rhs = jnp.concatenate([vh * bh, kb * eg[:, h:h + 1]], axis=1)
            sol = _dot(inv, rhs, HI)
            u, w = sol[:, 0:GDN_DH], sol[:, GDN_DH:2 * GDN_DH]
            s = s_ref[h]
            v_new = u - _dot(w, s, HI)
            attn = _dot_nt(qh, kh, HI) * lmat
            outs.append(_dot(qh * eg[:, h:h + 1], s, HI) + _dot(attn, v_new, HI))
            g_last = g[c - 1:c, h:h + 1]
            s_ref[h] = s * jnp.exp(g_last) + _dot_tn(kh * jnp.exp(g_last - gh), v_new, HI)
        o = jnp.concatenate(outs, axis=1)
        ms = _dot(o * o, group_ones * (1.0 / GDN_DH), HI)
        o_ref[0, pl.ds(base, c), :] = o * lax.rsqrt(ms + NORM_EPS) * ng_ref[...] * (gate * _sigmoid(gate))
        return carry

    lax.fori_loop(0, n_chunks, one_chunk, 0)

    @pl.when(pl.program_id(1) == pl.num_programs(1) - 1)
    def _():
        sout_ref[0] = s_ref[...]


def gdn_mixer(z, cw, buf0, alog, dtb, ng, s0, chunk):
    b, t, w = z.shape
    tb = _tile(t, SEQ_TILE)
    kern = functools.partial(_gdn_kernel, chunk=chunk, n_chunks=tb // chunk)
    return pl.pallas_call(
        kern,
        grid=(b, t // tb),
        in_specs=[pl.BlockSpec((1, tb, w), lambda i, j: (i, j, 0)),
                  pl.BlockSpec((GDN_CONV, 3 * GDN_W), lambda i, j: (0, 0)),
                  pl.BlockSpec((1, SUBLANES, 3 * GDN_W), lambda i, j: (i, 0, 0)),
                  pl.BlockSpec((1, LANES), lambda i, j: (0, 0)),
                  pl.BlockSpec((1, LANES), lambda i, j: (0, 0)),
                  pl.BlockSpec((1, GDN_W), lambda i, j: (0, 0)),
                  pl.BlockSpec((1, N_HEADS, GDN_DH, GDN_DH), lambda i, j: (i, 0, 0, 0))],
        out_specs=[pl.BlockSpec((1, tb, GDN_W), lambda i, j: (i, j, 0)),
                   pl.BlockSpec((1, N_HEADS, GDN_DH, GDN_DH), lambda i, j: (i, 0, 0, 0))],
        out_shape=[jax.ShapeDtypeStruct((b, t, GDN_W), F32),
                   jax.ShapeDtypeStruct((b, N_HEADS, GDN_DH, GDN_DH), F32)],
        scratch_shapes=[pltpu.VMEM((N_HEADS, GDN_DH, GDN_DH), F32), pltpu.VMEM((SUBLANES, 3 * GDN_W), F32)],
        compiler_params=_cparams(("parallel", "arbitrary")),
        name="gdn_mixer",
    )(z, cw, buf0, alog, dtb, ng, s0)


def _mla_prep_kernel(z_ref, cos_ref, sin_ref, qg_ref, kvg_ref, wn_ref, wr_ref, wk_ref,
                     q_ref, kc_ref, c_ref, kr_ref):
    z = z_ref[0]
    cq = z[:, 0:MLA_Q_LORA]
    ckv = z[:, MLA_Q_LORA:MLA_Q_LORA + MLA_KV_LORA]
    kr = z[:, MLA_Q_LORA + MLA_KV_LORA:MLA_Q_LORA + MLA_KV_LORA + LANES]
    cos, sin = cos_ref[...], sin_ref[...]
    cqn = _rms(cq, qg_ref[...]).astype(BF16)
    q_nope = _dot(cqn, wn_ref[...])
    q_rope = _dot(cqn, wr_ref[...])
    q_rope = q_rope * cos + pltpu.roll(q_rope, LANES // 2, 1) * sin
    k_rope = kr * cos + pltpu.roll(kr, LANES // 2, 1) * sin
    c = _rms(ckv, kvg_ref[...])
    c_ref[0] = c
    kr_ref[0] = k_rope
    kc_ref[0] = jnp.concatenate([c, k_rope], axis=1).astype(kc_ref.dtype)
    lane_head = (_iota(q_rope.shape, 1) // (MLA_ROPE // 2)) % N_HEADS
    for h in range(N_HEADS):
        q_lat = _dot(q_nope[:, h * MLA_NOPE:(h + 1) * MLA_NOPE].astype(BF16), wk_ref[h])
        q_r = jnp.where(lane_head == h, q_rope, 0.0)
        q_ref[0, 0, h] = (jnp.concatenate([q_lat, q_r], axis=1) * MLA_SCALE).astype(q_ref.dtype)


def mla_prep(z, cos, sin, qg, kvg, wn, wr, wk, act_dtype):
    b, t, w = z.shape
    tm = _tile(t, ROW_TILE)
    dk = MLA_KV_LORA + LANES
    return pl.pallas_call(
        _mla_prep_kernel,
        grid=(b, t // tm),
        in_specs=[pl.BlockSpec((1, tm, w), lambda i, j: (i, j, 0)),
                  pl.BlockSpec((tm, LANES), lambda i, j: (j, 0)),
                  pl.BlockSpec((tm, LANES), lambda i, j: (j, 0)),
                  pl.BlockSpec((1, MLA_Q_LORA), lambda i, j: (0, 0)),
                  pl.BlockSpec((1, MLA_KV_LORA), lambda i, j: (0, 0)),
                  pl.BlockSpec(wn.shape, lambda i, j: (0, 0)),
                  pl.BlockSpec(wr.shape, lambda i, j: (0, 0)),
                  pl.BlockSpec(wk.shape, lambda i, j: (0, 0, 0))],
        out_specs=[pl.BlockSpec((1, 1, N_HEADS, tm, dk), lambda i, j: (i, 0, 0, j, 0)),
                   pl.BlockSpec((1, tm, dk), lambda i, j: (i, j, 0)),
                   pl.BlockSpec((1, tm, MLA_KV_LORA), lambda i, j: (i, j, 0)),
                   pl.BlockSpec((1, tm, LANES), lambda i, j: (i, j, 0))],
        out_shape=[jax.ShapeDtypeStruct((b, 1, N_HEADS, t, dk), act_dtype),
                   jax.ShapeDtypeStruct((b, t, dk), act_dtype),
                   jax.ShapeDtypeStruct((b, t, MLA_KV_LORA), F32),
                   jax.ShapeDtypeStruct((b, t, LANES), F32)],
        compiler_params=_cparams(("parallel", "parallel")),
        name="mla_prep",
    )(z, cos, sin, qg, kvg, wn, wr, wk)


def _split3(x):
    hi = x.astype(BF16).astype(F32)
    r1 = x - hi
    mid = r1.astype(BF16).astype(F32)
    lo = (r1 - mid).astype(BF16).astype(F32)
    return hi, mid, lo


def _fox_prep_kernel(z_ref, bf_ref, q_ref, k_ref, v_ref, lf_ref, carry_ref):
    tm = z_ref.shape[1]

    @pl.when(pl.program_id(1) == 0)
    def _():
        carry_ref[...] = jnp.zeros_like(carry_ref)

    z = z_ref[0]
    w = N_HEADS * FOX_DH
    f = z[:, 3 * w:3 * w + LANES]
    logf = jnp.where(_iota((tm, LANES), 1) < N_HEADS, _log_sigmoid(f + bf_ref[...]), 0.0)
    lf_ref[0] = logf
    tril = (_iota((tm, tm), 1) <= _iota((tm, tm), 0)).astype(F32)
    cum = _dot(tril, logf, HI) + carry_ref[...]
    carry_ref[...] = cum[tm - 1:tm, :]
    hi, mid, lo = _split3(cum)
    lane = _iota((tm, FOX_DH), 1)
    ones = jnp.where(lane < 3, 1.0, 0.0)
    for h in range(N_HEADS):
        ch = [p[:, h:h + 1] for p in (hi, mid, lo)]
        c_q = jnp.where(lane == 0, ch[0], jnp.where(lane == 1, ch[1], jnp.where(lane == 2, ch[2], 0.0)))
        c_k = jnp.where(lane == 3, -ch[0], jnp.where(lane == 4, -ch[1], jnp.where(lane == 5, -ch[2], 0.0)))
        q_tail = c_q + jnp.where((lane >= 3) & (lane < 6), 1.0, 0.0)
        k_tail = c_k + ones
        qh = z[:, h * FOX_DH:(h + 1) * FOX_DH] * FOX_SCALE
        kh = z[:, w + h * FOX_DH:w + (h + 1) * FOX_DH]
        q_ref[0, h, 0] = jnp.concatenate([qh, q_tail], axis=1).astype(BF16)
        k_ref[0, h] = jnp.concatenate([kh, k_tail], axis=1).astype(BF16)
        v_ref[0, h] = z[:, 2 * w + h * FOX_DH:2 * w + (h + 1) * FOX_DH].astype(BF16)


def fox_prep(z, bf):
    b, t, w = z.shape
    tm = _tile(t, 256)
    dk = 2 * FOX_DH
    return pl.pallas_call(
        _fox_prep_kernel,
        grid=(b, t // tm),
        in_specs=[pl.BlockSpec((1, tm, w), lambda i, j: (i, j, 0)),
                  pl.BlockSpec((1, LANES), lambda i, j: (0, 0))],
        out_specs=[pl.BlockSpec((1, N_HEADS, 1, tm, dk), lambda i, j: (i, 0, 0, j, 0)),
                   pl.BlockSpec((1, N_HEADS, tm, dk), lambda i, j: (i, 0, j, 0)),
                   pl.BlockSpec((1, N_HEADS, tm, FOX_DH), lambda i, j: (i, 0, j, 0)),
                   pl.BlockSpec((1, tm, LANES), lambda i, j: (i, j, 0))],
        out_shape=[jax.ShapeDtypeStruct((b, N_HEADS, 1, t, dk), BF16),
                   jax.ShapeDtypeStruct((b, N_HEADS, t, dk), BF16),
                   jax.ShapeDtypeStruct((b, N_HEADS, t, FOX_DH), BF16),
                   jax.ShapeDtypeStruct((b, t, LANES), F32)],
        scratch_shapes=[pltpu.VMEM((1, LANES), F32)],
        compiler_params=_cparams(("parallel", "arbitrary")),
        name="fox_prep",
    )(z, bf)


def _flash_kernel(q_ref, k_ref, v_ref, o_ref, m_ref, l_ref, acc_ref, *, hs, tq):
    qi, ki = pl.program_id(2), pl.program_id(3)
    rows = hs * tq

    @pl.when(ki == 0)
    def _():
        m_ref[...] = jnp.full_like(m_ref, -jnp.inf)
        l_ref[...] = jnp.zeros_like(l_ref)
        acc_ref[...] = jnp.zeros_like(acc_ref)

    def update(masked):
        q = q_ref[0, 0].reshape(rows, q_ref.shape[-1])
        s = _dot_nt(q, k_ref[0, 0])
        if masked:
            qpos = _iota(s.shape, 0) % tq
            s = jnp.where(qpos >= _iota(s.shape, 1), s, -jnp.inf)
        m_old = m_ref[...]
        m_new = jnp.maximum(m_old, jnp.max(s, axis=-1, keepdims=True))
        alpha = jnp.exp(m_old - m_new)
        p = jnp.exp(s - m_new)
        l_ref[...] = alpha * l_ref[...] + jnp.sum(p, axis=-1, keepdims=True)
        acc_ref[...] = alpha * acc_ref[...] + _dot(p.astype(v_ref.dtype), v_ref[0, 0])
        m_ref[...] = m_new

    @pl.when(ki < qi)
    def _():
        update(False)

    @pl.when(ki == qi)
    def _():
        update(True)
        o = acc_ref[...] / l_ref[...]
        o_ref[0, 0] = o.reshape(hs, tq, o.shape[-1])


def flash_causal(q, k, v, dv):
    b, g, hs, t, dk = q.shape
    tq = _tile(t, ATTN_TILE)
    nq = t // tq
    kern = functools.partial(_flash_kernel, hs=hs, tq=tq)
    return pl.pallas_call(
        kern,
        grid=(b, g, nq, nq),
        in_specs=[pl.BlockSpec((1, 1, hs, tq, dk), lambda bi, gi, qi, ki: (bi, gi, 0, qi, 0)),
                  pl.BlockSpec((1, 1, tq, dk), lambda bi, gi, qi, ki: (bi, gi, jnp.minimum(ki, qi), 0)),
                  pl.BlockSpec((1, 1, tq, dv), lambda bi, gi, qi, ki: (bi, gi, jnp.minimum(ki, qi), 0))],
        out_specs=pl.BlockSpec((1, 1, hs, tq, dv), lambda bi, gi, qi, ki: (bi, gi, 0, qi, 0)),
        out_shape=jax.ShapeDtypeStruct((b, g, hs, t, dv), F32),
        scratch_shapes=[pltpu.VMEM((hs * tq, 1), F32), pltpu.VMEM((hs * tq, 1), F32),
                        pltpu.VMEM((hs * tq, dv), F32)],
        compiler_params=_cparams(("parallel", "parallel", "parallel", "arbitrary")),
        name="flash_causal",
    )(q, k, v)


def _online_update(s, v, m_ref, l_ref, acc_ref):
    m_old = m_ref[...]
    m_new = jnp.maximum(m_old, jnp.max(s, axis=-1, keepdims=True))
    alpha = jnp.exp(m_old - m_new)
    p = jnp.exp(s - m_new)
    l_ref[...] = alpha * l_ref[...] + jnp.sum(p, axis=-1, keepdims=True)
    acc_ref[...] = alpha * acc_ref[...] + _dot(p.astype(BF16), v)
    m_ref[...] = m_new


def _mla_decode_kernel(pt_ref, q_ref, kn_ref, *refs, n_new, pages):
    lat_refs, rope_refs = refs[:pages], refs[pages:2 * pages]
    o_ref, m_ref, l_ref, acc_ref = refs[2 * pages:]
    j = pl.program_id(1)
    rows = N_HEADS * n_new
    q = q_ref[0, 0].reshape(rows, q_ref.shape[-1])
    q_lat = q[:, 0:MLA_KV_LORA].astype(BF16)
    half = MLA_ROPE // 2
    sr, sc = _iota((LANES, MLA_ROPE), 0), _iota((LANES, MLA_ROPE), 1)
    compact = ((sr % half == sc % half) & ((sr >= LANES // 2) == (sc >= half))).astype(BF16)
    q_rope = _dot(q[:, MLA_KV_LORA:].astype(BF16), compact).astype(BF16)

    @pl.when(j == 0)
    def _():
        kn = kn_ref[0].astype(BF16)
        s = _dot_nt(q.astype(BF16), kn)
        s = jnp.where(_iota(s.shape, 0) % n_new >= _iota(s.shape, 1), s, -jnp.inf)
        m = jnp.max(s, axis=-1, keepdims=True)
        p = jnp.exp(s - m)
        m_ref[...] = m
        l_ref[...] = jnp.sum(p, axis=-1, keepdims=True)
        acc_ref[...] = _dot(p.astype(BF16), kn[:, 0:MLA_KV_LORA])

    for i in range(pages):
        c = lat_refs[i][0].astype(BF16)
        r = rope_refs[i][0].astype(BF16)
        s = _dot_nt(q_lat, c) + _dot_nt(q_rope, r)
        _online_update(s, c, m_ref, l_ref, acc_ref)

    @pl.when(j == pl.num_programs(1) - 1)
    def _():
        o = acc_ref[...] / l_ref[...]
        o_ref[0, 0] = o.reshape(N_HEADS, n_new, MLA_KV_LORA)


def mla_decode(q, kc_new, lat_pool, rope_pool, page_table):
    b, _, _, n_new, dk = q.shape
    n_pages = page_table.shape[1]
    pages = _tile(n_pages, DEC_PAGES)

    def page_map(i):
        return lambda bi, j, pt: (pt[bi, j * pages + i], 0, 0)

    in_specs = [pl.BlockSpec((1, 1, N_HEADS, n_new, dk), lambda bi, j, pt: (bi, 0, 0, 0, 0)),
                pl.BlockSpec((1, n_new, dk), lambda bi, j, pt: (bi, 0, 0))]
    in_specs += [pl.BlockSpec((1, PAGE, MLA_KV_LORA), page_map(i)) for i in range(pages)]
    in_specs += [pl.BlockSpec((1, PAGE, MLA_ROPE), page_map(i)) for i in range(pages)]
    rows = N_HEADS * n_new
    kern = functools.partial(_mla_decode_kernel, n_new=n_new, pages=pages)
    return pl.pallas_call(
        kern,
        grid_spec=pltpu.PrefetchScalarGridSpec(
            num_scalar_prefetch=1,
            grid=(b, n_pages // pages),
            in_specs=in_specs,
            out_specs=pl.BlockSpec((1, 1, N_HEADS, n_new, MLA_KV_LORA), lambda bi, j, pt: (bi, 0, 0, 0, 0)),
            scratch_shapes=[pltpu.VMEM((rows, 1), F32), pltpu.VMEM((rows, 1), F32),
                            pltpu.VMEM((rows, MLA_KV_LORA), F32)]),
        out_shape=jax.ShapeDtypeStruct((b, 1, N_HEADS, n_new, MLA_KV_LORA), F32),
        compiler_params=_cparams(("parallel", "arbitrary")),
        name="mla_decode",
    )(page_table, q, kc_new, *([lat_pool] * pages), *([rope_pool] * pages))


def _fox_decode_kernel(pt_ref, z_ref, bf_ref, wafter_ref, wtot_ref, *refs, n_new, pages):
    k_refs, v_refs, f_refs = refs[:pages], refs[pages:2 * pages], refs[2 * pages:3 * pages]
    o_ref, lf_ref, m_ref, l_ref, acc_ref, carry_ref = refs[3 * pages:]
    j = pl.program_id(1)
    w = N_HEADS * FOX_DH
    rows = N_HEADS * n_new
    z = z_ref[0]
    q = z[:, 0:w] * FOX_SCALE
    f = z[:, 3 * w:3 * w + LANES]
    logf = jnp.where(_iota((n_new, LANES), 1) < N_HEADS, _log_sigmoid(f + bf_ref[...]), 0.0)
    tril = (_iota((n_new, n_new), 1) <= _iota((n_new, n_new), 0)).astype(F32)
    triu = (_iota((n_new, n_new), 0) <= _iota((n_new, n_new), 1)).astype(F32)
    cn = _dot(tril, logf, HI)
    cn_t = _dot_tn(logf, triu, HI)
    block = _iota((rows, w), 0) // n_new == _iota((rows, w), 1) // FOX_DH
    q_bd = jnp.where(block, jnp.concatenate([q] * N_HEADS, axis=0), 0.0).astype(BF16)
    cn_col = jnp.concatenate([cn[:, h:h + 1] for h in range(N_HEADS)], axis=0)

    @pl.when(j == 0)
    def _():
        lf_ref[0] = logf
        kn = z[:, w:2 * w].astype(BF16)
        vn = z[:, 2 * w:3 * w].astype(BF16)
        cn_row = jnp.concatenate([jnp.broadcast_to(cn_t[h:h + 1, :], (n_new, n_new)) for h in range(N_HEADS)], axis=0)
        s = _dot_nt(q_bd, kn) + cn_col - cn_row
        s = jnp.where(_iota(s.shape, 0) % n_new >= _iota(s.shape, 1), s, -jnp.inf)
        m = jnp.max(s, axis=-1, keepdims=True)
        p = jnp.exp(s - m)
        m_ref[...] = m
        l_ref[...] = jnp.sum(p, axis=-1, keepdims=True)
        acc_ref[...] = _dot(p.astype(BF16), vn)
        carry_ref[...] = jnp.zeros_like(carry_ref)

    for i in range(pages):
        lp = jnp.broadcast_to(f_refs[i][0], (SUBLANES, N_HEADS * PAGE))
        after = _dot(lp, wafter_ref[...], HI) + carry_ref[...]
        carry_ref[...] = carry_ref[...] + _dot(lp, wtot_ref[...], HI)
        bias = jnp.concatenate([jnp.broadcast_to(after[0:1, h * PAGE:(h + 1) * PAGE], (n_new, PAGE))
                                for h in range(N_HEADS)], axis=0)
        s = _dot_nt(q_bd, k_refs[i][0].astype(BF16)) + cn_col + bias
        _online_update(s, v_refs[i][0].astype(BF16), m_ref, l_ref, acc_ref)

    @pl.when(j == pl.num_programs(1) - 1)
    def _():
        o = jnp.where(block, acc_ref[...] / l_ref[...], 0.0)
        out = o[0:n_new, :]
        for h in range(1, N_HEADS):
            out = out + o[h * n_new:(h + 1) * n_new, :]
        o_ref[0] = out


def fox_decode(z, bf, k_pool, v_pool, f_pool, page_table):
    b, n_new, zw = z.shape
    n_pages = page_table.shape[1]
    pages = _tile(n_pages, DEC_PAGES)
    w = N_HEADS * FOX_DH
    fl = N_HEADS * PAGE
    fi, fo = jnp.arange(fl)[:, None], jnp.arange(fl)[None, :]
    same_head = (fi % N_HEADS) == (fo // PAGE)
    w_after = (same_head & ((fi // N_HEADS) > (fo % PAGE))).astype(F32)
    w_total = same_head.astype(F32)

    def page_map(i):
        return lambda bi, j, pt: (pt[bi, n_pages - 1 - (j * pages + i)], 0, 0)

    in_specs = [pl.BlockSpec((1, n_new, zw), lambda bi, j, pt: (bi, 0, 0)),
                pl.BlockSpec((1, LANES), lambda bi, j, pt: (0, 0)),
                pl.BlockSpec((fl, fl), lambda bi, j, pt: (0, 0)),
                pl.BlockSpec((fl, fl), lambda bi, j, pt: (0, 0))]
    in_specs += [pl.BlockSpec((1, PAGE, w), page_map(i)) for i in range(pages)]
    in_specs += [pl.BlockSpec((1, PAGE, w), page_map(i)) for i in range(pages)]
    in_specs += [pl.BlockSpec((1, 1, fl), page_map(i)) for i in range(pages)]
    rows = N_HEADS * n_new
    kern = functools.partial(_fox_decode_kernel, n_new=n_new, pages=pages)
    return pl.pallas_call(
        kern,
        grid_spec=pltpu.PrefetchScalarGridSpec(
            num_scalar_prefetch=1,
            grid=(b, n_pages // pages),
            in_specs=in_specs,
            out_specs=[pl.BlockSpec((1, n_new, w), lambda bi, j, pt: (bi, 0, 0)),
                       pl.BlockSpec((1, n_new, LANES), lambda bi, j, pt: (bi, 0, 0))],
            scratch_shapes=[pltpu.VMEM((rows, 1), F32), pltpu.VMEM((rows, 1), F32),
                            pltpu.VMEM((rows, w), F32), pltpu.VMEM((SUBLANES, fl), F32)]),
        out_shape=[jax.ShapeDtypeStruct((b, n_new, w), F32), jax.ShapeDtypeStruct((b, n_new, LANES), F32)],
        compiler_params=_cparams(("parallel", "arbitrary")),
        name="fox_decode",
    )(page_table, z, bf, w_after, w_total, *([k_pool] * pages), *([v_pool] * pages), *([f_pool] * pages))


def _merge_kernel(x_ref, oa_ref, olat_ref, oc_ref, od_ref, gate_ref, wuv_ref, wb_ref, wo_ref, h_ref):
    d = x_ref.shape[1]
    olat = olat_ref[...]
    ob = jnp.concatenate([_dot(olat[:, h * MLA_KV_LORA:(h + 1) * MLA_KV_LORA].astype(BF16), wuv_ref[h])
                          for h in range(N_HEADS)], axis=1)
    acc = None
    for n, o in enumerate((oa_ref[...], ob, oc_ref[...], od_ref[...])):
        term = _sigmoid(gate_ref[:, n * d:(n + 1) * d]) * _dot(o.astype(BF16), wb_ref[n])
        acc = term if acc is None else acc + term
    h_ref[...] = x_ref[...] + _dot(acc.astype(BF16), wo_ref[...])


def merge(x, oa, olat, oc, od, gate, wuv, wb, wo):
    n, d = x.shape
    tm = _tile(n, ROW_TILE)
    row = lambda width: pl.BlockSpec((tm, width), lambda i: (i, 0))
    return pl.pallas_call(
        _merge_kernel,
        grid=(n // tm,),
        in_specs=[row(d), row(BRANCH_W), row(N_HEADS * MLA_KV_LORA), row(BRANCH_W), row(BRANCH_W), row(4 * d),
                  pl.BlockSpec(wuv.shape, lambda i: (0, 0, 0)),
                  pl.BlockSpec(wb.shape, lambda i: (0, 0, 0)),
                  pl.BlockSpec(wo.shape, lambda i: (0, 0))],
        out_specs=row(d),
        out_shape=jax.ShapeDtypeStruct((n, d), F32),
        compiler_params=_cparams(("parallel",)),
        name="merge",
    )(x, oa, olat, oc, od, gate, wuv, wb, wo)


def _peer_select_kernel(h_ref, g_ref, wq_ref, sk_ref, xn_ref, idx_ref, gate_ref, q_ref, sc_ref):
    hd = pl.program_id(1)
    tm = h_ref.shape[0]
    qd = 2 * PEER_HALF
    k2 = PEER_TOPK * PEER_TOPK

    @pl.when(hd == 0)
    def _():
        xn = _rms(h_ref[...], g_ref[...])
        xn_ref[...] = xn
        q = _dot(xn, wq_ref[...], HI)
        for i in range(PEER_HEADS):
            q_ref[i] = q[:, i * qd:(i + 1) * qd]
        idx_ref[...] = jnp.zeros_like(idx_ref)
        sc_ref[...] = jnp.zeros_like(sc_ref)

    qh = q_ref[hd]
    lane = _iota((tm, PEER_NKEYS), 1)
    lane2 = _iota((tm, k2), 1)
    cand_s = jnp.zeros((tm, k2), F32)
    cand_i = jnp.zeros((tm, k2), I32)
    for p in range(2):
        sub = _dot_nt(qh[:, p * PEER_HALF:(p + 1) * PEER_HALF], sk_ref[0, p], HI)
        pos = (lane2 // PEER_TOPK) if p == 0 else (lane2 % PEER_TOPK)
        mult = PEER_NKEYS if p == 0 else 1

        def pick(a, carry, pos=pos, mult=mult):
            work, cs, ci = carry
            m = jnp.max(work, axis=-1, keepdims=True)
            am = jnp.min(jnp.where(work == m, lane, PEER_NKEYS), axis=-1, keepdims=True)
            sel = pos == a
            cs = jnp.where(sel, cs + m, cs)
            ci = jnp.where(sel, ci + am * mult, ci)
            return jnp.where(lane == am, -jnp.inf, work), cs, ci

        _, cand_s, cand_i = lax.fori_loop(0, PEER_TOPK, pick, (sub, cand_s, cand_i))

    def best(kk, carry):
        work, sc, ix = carry
        m = jnp.max(work, axis=-1, keepdims=True)
        am = jnp.min(jnp.where(work == m, lane2, k2), axis=-1, keepdims=True)
        hit = lane2 == am
        expert = jnp.max(jnp.where(hit, cand_i, -1), axis=-1, keepdims=True)
        out = lane == hd * PEER_TOPK + kk
        return jnp.where(hit, -jnp.inf, work), jnp.where(out, m, sc), jnp.where(out, expert, ix)

    _, sc, ix = lax.fori_loop(0, PEER_TOPK, best, (cand_s, sc_ref[...], idx_ref[...]))
    sc_ref[...] = sc
    idx_ref[...] = ix

    @pl.when(hd == pl.num_programs(1) - 1)
    def _():
        grp = lane // PEER_TOPK
        mx = jnp.zeros((tm, PEER_NKEYS), F32)
        for i in range(PEER_HEADS):
            mx = jnp.where(grp == i, jnp.max(jnp.where(grp == i, sc, -jnp.inf), axis=-1, keepdims=True), mx)
        e = jnp.exp(sc - mx)
        den = _dot(e, _group_ones(PEER_NKEYS, PEER_TOPK), HI)
        gate_ref[...] = e / den


def peer_select(h, g, wq, sk):
    n, d = h.shape
    tm = _tile(n, SELECT_TILE)
    nk = PEER_HEADS * PEER_TOPK
    return pl.pallas_call(
        _peer_select_kernel,
        grid=(n // tm, PEER_HEADS),
        in_specs=[pl.BlockSpec((tm, d), lambda i, j: (i, 0)),
                  pl.BlockSpec((1, d), lambda i, j: (0, 0)),
                  pl.BlockSpec(wq.shape, lambda i, j: (0, 0)),
                  pl.BlockSpec((1, 2, PEER_NKEYS, PEER_HALF), lambda i, j: (j, 0, 0, 0))],
        out_specs=[pl.BlockSpec((tm, d), lambda i, j: (i, 0)),
                   pl.BlockSpec((tm, nk), lambda i, j: (i, 0)),
                   pl.BlockSpec((tm, nk), lambda i, j: (i, 0))],
        out_shape=[jax.ShapeDtypeStruct((n, d), F32), jax.ShapeDtypeStruct((n, nk), I32),
                   jax.ShapeDtypeStruct((n, nk), F32)],
        scratch_shapes=[pltpu.VMEM((PEER_HEADS, tm, 2 * PEER_HALF), F32), pltpu.VMEM((tm, nk), F32)],
        compiler_params=_cparams(("parallel", "arbitrary")),
        name="peer_select",
    )(h, g, wq, sk)


def _gelu(x):
    return 0.5 * x * (1.0 + lax.erf(x * (2.0 ** -0.5)))


def _peer_gather_kernel(idx_ref, idxn_ref, gate_ref, xn_ref, h_ref, uv_hbm, y_ref, buf, sem, wb_ref, *, tokens):
    i = pl.program_id(0)
    n = pl.num_programs(0)
    nk = PEER_HEADS * PEER_TOPK
    picks = tokens * nk

    def issue(ids_ref, slot):
        def body(r, carry):
            e = ids_ref[r // nk, r % nk]
            pltpu.make_async_copy(uv_hbm.at[e], buf.at[slot, r], sem.at[slot]).start()
            return carry
        lax.fori_loop(0, picks, body, 0, unroll=8)

    @pl.when(i == 0)
    def _():
        issue(idx_ref, 0)

    @pl.when(i + 1 < n)
    def _():
        issue(idxn_ref, (i + 1) % 2)

    slot = i % 2
    pltpu.make_async_copy(buf.at[slot], buf.at[slot], sem.at[slot]).wait()

    lane = _iota((SUBLANES, nk), 1)
    eye = _iota((nk, nk), 0) == _iota((nk, nk), 1)

    def one_token(t, carry):
        xt = xn_ref[t]

        def u_body(jj, m):
            p = buf[slot, t * nk + jj, 0:SUBLANES, :] * xt
            return jnp.where(lane == jj, jnp.sum(p, axis=1, keepdims=True), m)

        m = lax.fori_loop(0, nk, u_body, jnp.zeros((SUBLANES, nk), F32), unroll=8)
        act = jnp.sum(m, axis=0, keepdims=True)
        wgt = gate_ref[pl.ds(t, 1), :] * _gelu(act)
        col = jnp.sum(jnp.where(eye, jnp.broadcast_to(wgt, (nk, nk)), 0.0), axis=1, keepdims=True)
        wb_ref[...] = jnp.broadcast_to(col, (nk, nk))

        def v_body(jj, acc):
            return acc + wb_ref[pl.ds(jj, 1), :] * buf[slot, t * nk + jj, SUBLANES:2 * SUBLANES, :]

        y = lax.fori_loop(0, nk, v_body, jnp.zeros((SUBLANES, LANES), F32), unroll=8)
        y_ref[t] = h_ref[t] + y
        return carry

    lax.fori_loop(0, tokens, one_token, 0)


def peer_gather(idx, gate, xn, h, uv):
    n, d = h.shape
    tb = _tile(n, GATHER_TOKENS)
    nk = PEER_HEADS * PEER_TOPK
    steps = n // tb
    xn3 = xn.reshape(n, SUBLANES, d // SUBLANES)
    h3 = h.reshape(n, SUBLANES, d // SUBLANES)
    tile3 = pl.BlockSpec((tb, SUBLANES, d // SUBLANES), lambda i: (i, 0, 0))
    kern = functools.partial(_peer_gather_kernel, tokens=tb)
    y = pl.pallas_call(
        kern,
        grid=(steps,),
        in_specs=[pl.BlockSpec((tb, nk), lambda i: (i, 0), memory_space=pltpu.SMEM),
                  pl.BlockSpec((tb, nk), lambda i: (jnp.minimum(i + 1, steps - 1), 0), memory_space=pltpu.SMEM),
                  pl.BlockSpec((tb, nk), lambda i: (i, 0)),
                  tile3, tile3,
                  pl.BlockSpec(memory_space=pl.ANY)],
        out_specs=tile3,
        out_shape=jax.ShapeDtypeStruct(h3.shape, F32),
        scratch_shapes=[pltpu.VMEM((2, tb * nk, 2 * SUBLANES, d // SUBLANES), F32),
                        pltpu.SemaphoreType.DMA((2,)),
                        pltpu.VMEM((nk, nk), F32)],
        compiler_params=_cparams(("arbitrary",)),
        name="peer_gather",
    )(idx, idx, gate, xn3, h3, uv)
    return y.reshape(n, d)


def _pad_cols(w, width):
    return jnp.pad(w, ((0, 0), (0, width - w.shape[1])))


def _layer_weights(l, p):
    d = p['w_in'].shape[1]
    w_in = p['w_in'][l]
    hk, hv = N_HEADS * GLA_DK, N_HEADS * GLA_DV
    off = 0
    cols = {}
    for name, width in (('gla_q', hk), ('gla_k', hk), ('gla_v', hv), ('gla_r', hv), ('gla_a', GLA_LR),
                        ('mla_cq', MLA_Q_LORA), ('mla_ckv', MLA_KV_LORA), ('mla_kr', MLA_ROPE),
                        ('fox_q', GDN_W), ('fox_k', GDN_W), ('fox_v', GDN_W), ('fox_f', N_HEADS),
                        ('gdn_qkv', 3 * GDN_W), ('gdn_a', N_HEADS), ('gdn_b', N_HEADS), ('gdn_g', GDN_W),
                        ('gate', 4 * d)):
        cols[name] = w_in[:, off:off + width]
        off += width
    assert off == w_in.shape[1]
    half = MLA_ROPE // 2
    kr = cols['mla_kr']
    kr_tiled = jnp.concatenate([jnp.tile(kr[:, :half], (1, N_HEADS)), jnp.tile(kr[:, half:], (1, N_HEADS))], axis=1)
    lw = {
        'w_gla': jnp.concatenate([cols['gla_q'], cols['gla_k'], cols['gla_v'], cols['gla_r'],
                                  _pad_cols(cols['gla_a'], LANES)], axis=1).astype(BF16),
        'w_mla': jnp.concatenate([cols['mla_cq'], cols['mla_ckv'], kr_tiled], axis=1).astype(BF16),
        'w_fox': jnp.concatenate([cols['fox_q'], cols['fox_k'], cols['fox_v'],
                                  _pad_cols(cols['fox_f'], LANES)], axis=1).astype(BF16),
        'w_gdn': jnp.concatenate([cols['gdn_qkv'], cols['gdn_g'],
                                  _pad_cols(jnp.concatenate([cols['gdn_a'], cols['gdn_b']], axis=1), LANES)],
                                 axis=1).astype(BF16),
        'w_gate': cols['gate'].astype(BF16),
    }
    lw['gla_wa'] = jnp.pad(p['gla_w_a2'][l], ((0, LANES - GLA_LR), (0, 0)))
    lw['gla_ba'] = p['gla_b_a'][l][None, :]
    lw['gla_ng'] = jnp.tile(p['gla_norm_g'][l], N_HEADS)[None, :]
    w_uq = p['mla_w_uq'][l]
    lw['mla_wn'] = w_uq[:, :, :MLA_NOPE].reshape(MLA_Q_LORA, N_HEADS * MLA_NOPE).astype(BF16)
    lw['mla_wr'] = jnp.concatenate([w_uq[:, :, MLA_NOPE:MLA_NOPE + half].reshape(MLA_Q_LORA, N_HEADS * half),
                                    w_uq[:, :, MLA_NOPE + half:].reshape(MLA_Q_LORA, N_HEADS * half)],
                                   axis=1).astype(BF16)
    lw['mla_wk'] = jnp.transpose(p['mla_w_uk'][l], (1, 2, 0)).astype(BF16)
    lw['mla_wuv'] = jnp.transpose(p['mla_w_uv'][l], (1, 0, 2)).astype(BF16)
    lw['mla_qg'] = p['mla_qnorm_g'][l][None, :]
    lw['mla_kvg'] = p['mla_kvnorm_g'][l][None, :]
    lw['fox_bf'] = _pad_cols(p['fox_b_f'][l][None, :], LANES)
    lw['gdn_cw'] = p['gdn_conv_w'][l]
    lw['gdn_alog'] = _pad_cols(p['gdn_a_log'][l][None, :], LANES)
    lw['gdn_dtb'] = _pad_cols(p['gdn_dt_bias'][l][None, :], LANES)
    lw['gdn_ng'] = jnp.tile(p['gdn_norm_g'][l], N_HEADS)[None, :]
    lw['wb'] = p['w_branch'][l].astype(BF16)
    lw['wo'] = p['w_out'][l].astype(BF16)
    lw['norm1_g'] = p['norm1_g'][l][None, :]
    lw['norm2_g'] = p['norm2_g'][l][None, :]
    lw['peer_wq'] = p['peer_w_q'][l]
    lw['peer_sk'] = p['peer_subkeys'][l]
    n_exp = p['peer_u'].shape[1]
    lw['peer_uv'] = jnp.concatenate([p['peer_u'][l].reshape(n_exp, SUBLANES, d // SUBLANES),
                                     p['peer_v'][l].reshape(n_exp, SUBLANES, d // SUBLANES)], axis=1)
    return lw


def _rope_tables(pos):
    half = MLA_ROPE // 2
    inv = ROPE_BASE ** (-jnp.arange(half, dtype=F32) / half)
    ang = pos.astype(F32)[:, None] * inv[None, :]
    cos, sin = jnp.cos(ang), jnp.sin(ang)
    cos_t = jnp.tile(cos, (1, 2 * N_HEADS))
    sin_t = jnp.concatenate([jnp.tile(-sin, (1, N_HEADS)), jnp.tile(sin, (1, N_HEADS))], axis=1)
    return cos_t, sin_t


def _gla_state_to_kernel(s):
    b = s.shape[0]
    eye = jnp.eye(N_HEADS, dtype=s.dtype)
    return jnp.einsum('bhdv,hg->bhvgd', s, eye).reshape(b, N_HEADS * GLA_DV, N_HEADS * GLA_DK)


def _gla_state_from_kernel(st):
    b = st.shape[0]
    s5 = st.reshape(b, N_HEADS, GLA_DV, N_HEADS, GLA_DK)
    diag = jnp.stack([s5[:, h, :, h, :] for h in range(N_HEADS)], axis=1)
    return jnp.swapaxes(diag, 2, 3)


def _token_mix(x, lw, pos, past, l):
    b, t, d = x.shape
    n = b * t
    x2 = x.reshape(n, d)
    g1 = lw['norm1_g']
    z_gla = norm_proj(x2, g1, lw['w_gla']).reshape(b, t, -1)
    z_mla = norm_proj(x2, g1, lw['w_mla']).reshape(b, t, -1)
    z_fox = norm_proj(x2, g1, lw['w_fox']).reshape(b, t, -1)
    z_gdn = norm_proj(x2, g1, lw['w_gdn']).reshape(b, t, -1)
    z_gate = norm_proj(x2, g1, lw['w_gate'])

    if past is None:
        gla_s0 = jnp.zeros((b, N_HEADS * GLA_DV, N_HEADS * GLA_DK), F32)
    else:
        gla_s0 = _gla_state_to_kernel(past['state_gla'][l])
    o_a, gla_st = gla_mixer(z_gla, lw['gla_wa'], lw['gla_ba'], lw['gla_ng'], gla_s0, min(GLA_CHUNK, t))
    gla_s = _gla_state_from_kernel(gla_st)

    assert t >= GDN_CONV - 1
    if past is None:
        gdn_s0 = jnp.zeros((b, N_HEADS, GDN_DH, GDN_DH), F32)
        buf0 = jnp.zeros((b, SUBLANES, 3 * GDN_W), F32)
    else:
        gdn_s0 = past['state_gdn'][l]
        buf0 = jnp.pad(past['state_gdn_conv'][l], ((0, 0), (SUBLANES - (GDN_CONV - 1), 0), (0, 0)))
    o_d, gdn_s = gdn_mixer(z_gdn, lw['gdn_cw'], buf0, lw['gdn_alog'], lw['gdn_dtb'], lw['gdn_ng'], gdn_s0,
                           min(GDN_CHUNK, t))
    gdn_buf = z_gdn[:, t - (GDN_CONV - 1):, :3 * GDN_W]

    cos_t, sin_t = _rope_tables(pos)
    act_dtype = BF16 if past is None else F32
    q_m, kc, c_lat, kr_t = mla_prep(z_mla, cos_t, sin_t, lw['mla_qg'], lw['mla_kvg'], lw['mla_wn'], lw['mla_wr'],
                                    lw['mla_wk'], act_dtype)
    half = MLA_ROPE // 2
    k_rope = jnp.concatenate([kr_t[..., :half], kr_t[..., LANES // 2:LANES // 2 + half]], axis=-1)
    if past is None:
        o_lat = flash_causal(q_m, kc[:, None], kc[:, None], MLA_KV_LORA)
    else:
        o_lat = mla_decode(q_m, kc, past['cache_mla_latent'][l], past['cache_mla_rope'][l], past['page_table'])
    o_lat = jnp.transpose(o_lat[:, 0], (0, 2, 1, 3)).reshape(n, N_HEADS * MLA_KV_LORA)

    w = N_HEADS * FOX_DH
    fk = z_fox[..., w:2 * w].reshape(b, t, N_HEADS, FOX_DH)
    fv = z_fox[..., 2 * w:3 * w].reshape(b, t, N_HEADS, FOX_DH)
    if past is None:
        q_f, k_f, v_f, lf = fox_prep(z_fox, lw['fox_bf'])
        o_c = flash_causal(q_f, k_f, v_f, FOX_DH)
        o_c = jnp.transpose(o_c[:, :, 0], (0, 2, 1, 3)).reshape(n, w)
    else:
        n_pool = past['cache_fox_k'].shape[1]
        o_c, lf = fox_decode(z_fox, lw['fox_bf'],
                             past['cache_fox_k'][l].reshape(n_pool, PAGE, w),
                             past['cache_fox_v'][l].reshape(n_pool, PAGE, w),
                             past['cache_fox_logf'][l].reshape(n_pool, 1, PAGE * N_HEADS),
                             past['page_table'])
        o_c = o_c.reshape(n, w)
    logf = lf[..., :N_HEADS]

    h = merge(x2, o_a.reshape(n, -1), o_lat, o_c, o_d.reshape(n, -1), z_gate, lw['mla_wuv'], lw['wb'], lw['wo'])
    state = {'mla_latent': c_lat, 'mla_rope': k_rope, 'fox_k': fk, 'fox_v': fv, 'fox_logf': logf,
             'gla': gla_s, 'gdn': gdn_s, 'gdn_conv': gdn_buf}
    return h, state


def _run_trunk(x, layer_weights, final_g, pos, past):
    b, t, d = x.shape
    states = []
    for l, lw in enumerate(layer_weights):
        h, st = _token_mix(x, lw, pos, past, l)
        xn, idx, gate = peer_select(h, lw['norm2_g'], lw['peer_wq'], lw['peer_sk'])
        x = peer_gather(idx, gate, xn, h, lw['peer_uv']).reshape(b, t, d)
        states.append(st)
    stacked = {name: jnp.stack([st[name] for st in states]) for name in states[0]}
    y = rmsnorm(x.reshape(b * t, d), final_g[None, :]).reshape(b, t, d)
    return y, stacked


def kernel(x_prompt, x_sample, cache_mla_latent, cache_mla_rope, cache_fox_k, cache_fox_v, cache_fox_logf,
           state_gla, state_gdn, state_gdn_conv, page_table,
           norm1_g, w_in, gla_w_a2, gla_b_a, gla_norm_g, mla_qnorm_g, mla_w_uq, mla_kvnorm_g, mla_w_uk, mla_w_uv,
           fox_b_f, gdn_conv_w, gdn_a_log, gdn_dt_bias, gdn_norm_g, w_branch, w_out, norm2_g,
           peer_w_q, peer_subkeys, peer_u, peer_v, final_norm_g):
    params = {
        'norm1_g': norm1_g, 'w_in': w_in, 'gla_w_a2': gla_w_a2, 'gla_b_a': gla_b_a, 'gla_norm_g': gla_norm_g,
        'mla_qnorm_g': mla_qnorm_g, 'mla_w_uq': mla_w_uq, 'mla_kvnorm_g': mla_kvnorm_g, 'mla_w_uk': mla_w_uk,
        'mla_w_uv': mla_w_uv, 'fox_b_f': fox_b_f, 'gdn_conv_w': gdn_conv_w, 'gdn_a_log': gdn_a_log,
        'gdn_dt_bias': gdn_dt_bias, 'gdn_norm_g': gdn_norm_g, 'w_branch': w_branch, 'w_out': w_out,
        'norm2_g': norm2_g, 'peer_w_q': peer_w_q, 'peer_subkeys': peer_subkeys, 'peer_u': peer_u, 'peer_v': peer_v,
    }
    past = {
        'cache_mla_latent': cache_mla_latent, 'cache_mla_rope': cache_mla_rope, 'cache_fox_k': cache_fox_k,
        'cache_fox_v': cache_fox_v, 'cache_fox_logf': cache_fox_logf, 'state_gla': state_gla,
        'state_gdn': state_gdn, 'state_gdn_conv': state_gdn_conv, 'page_table': page_table,
    }
    depth = w_in.shape[0]
    layer_weights = [_layer_weights(l, params) for l in range(depth)]
    past_len = page_table.shape[1] * cache_mla_latent.shape[2]
    pos_prompt = jnp.arange(x_prompt.shape[1], dtype=I32)
    pos_sample = past_len + jnp.arange(x_sample.shape[1], dtype=I32)
    y_p, st_p = _run_trunk(x_prompt, layer_weights, final_norm_g, pos_prompt, None)
    y_s, st_s = _run_trunk(x_sample, layer_weights, final_norm_g, pos_sample, past)
    return (y_p, y_s,
            st_p['mla_latent'], st_s['mla_latent'], st_p['mla_rope'], st_s['mla_rope'],
            st_p['fox_k'], st_s['fox_k'], st_p['fox_v'], st_s['fox_v'],
            st_p['fox_logf'], st_s['fox_logf'], st_p['gla'], st_s['gla'],
            st_p['gdn'], st_s['gdn'], st_p['gdn_conv'], st_s['gdn_conv'])
```

```python
import functools
import math

import jax
import jax.numpy as jnp
from jax import lax
from jax.experimental import pallas as pl
from jax.experimental.pallas import tpu as pltpu

F32 = jnp.float32
BF16 = jnp.bfloat16
I32 = jnp.int32
HI = lax.Precision.HIGHEST

NORM_EPS = 1e-6
N_HEADS = 4
GLA_DK, GLA_DV, GLA_LR, GLA_TAU, GLA_CHUNK = 32, 64, 16, 16.0, 64
MLA_NOPE, MLA_ROPE, MLA_KV_LORA, MLA_Q_LORA, MLA_V = 64, 32, 128, 256, 64
MLA_SCALE = (MLA_NOPE + MLA_ROPE) ** -0.5
ROPE_BASE = 10000.0
FOX_DH = 64
FOX_SCALE = FOX_DH ** -0.5
GDN_DH, GDN_CONV, GDN_CHUNK = 64, 4, 64
GDN_W = N_HEADS * GDN_DH
PEER_HEADS, PEER_NKEYS, PEER_HALF, PEER_TOPK = 8, 128, 64, 16
BRANCH_W = 256
PAGE = 128

LANES = 128
SUBLANES = 8
VMEM_LIMIT = 56 * 1024 * 1024

ROW_TILE = 512
SEQ_TILE = 512
ATTN_TILE = 512
SELECT_TILE = 256
GATHER_TOKENS = 8
DEC_PAGES = 16


def _tile(n, pref):
    t = min(n, pref)
    assert n % t == 0, (n, pref)
    return t


def _cparams(sem):
    return pltpu.CompilerParams(dimension_semantics=sem, vmem_limit_bytes=VMEM_LIMIT)


def _rms(x, g):
    return x * lax.rsqrt(jnp.mean(x * x, axis=-1, keepdims=True) + NORM_EPS) * g


def _softplus(x):
    return jnp.maximum(x, 0.0) + jnp.log1p(jnp.exp(-jnp.abs(x)))


def _log_sigmoid(x):
    return -_softplus(-x)


def _sigmoid(x):
    return 1.0 / (1.0 + jnp.exp(-x))


def _iota(shape, axis):
    return lax.broadcasted_iota(I32, shape, axis)


def _dot(a, b, precision=None):
    return jnp.dot(a, b, precision=precision, preferred_element_type=F32)


def _dot_nt(a, b, precision=None):
    return lax.dot_general(a, b, (((1,), (1,)), ((), ())), precision=precision, preferred_element_type=F32)


def _dot_tn(a, b, precision=None):
    return lax.dot_general(a, b, (((0,), (0,)), ((), ())), precision=precision, preferred_element_type=F32)


def _group_ones(n, group, dtype=F32):
    return (_iota((n, n), 0) // group == _iota((n, n), 1) // group).astype(dtype)


def _norm_proj_kernel(x_ref, g_ref, w_ref, o_ref, xn_ref):
    @pl.when(pl.program_id(1) == 0)
    def _():
        xn_ref[...] = _rms(x_ref[...], g_ref[...]).astype(BF16)

    o_ref[...] = _dot(xn_ref[...], w_ref[...])


def norm_proj(x, g, w):
    n, d = x.shape
    m = w.shape[1]
    tm = _tile(n, ROW_TILE)
    tn = m if m <= 1280 else _tile(m, 1024)
    return pl.pallas_call(
        _norm_proj_kernel,
        grid=(n // tm, m // tn),
        in_specs=[pl.BlockSpec((tm, d), lambda i, j: (i, 0)),
                  pl.BlockSpec((1, d), lambda i, j: (0, 0)),
                  pl.BlockSpec((d, tn), lambda i, j: (0, j))],
        out_specs=pl.BlockSpec((tm, tn), lambda i, j: (i, j)),
        out_shape=jax.ShapeDtypeStruct((n, m), F32),
        scratch_shapes=[pltpu.VMEM((tm, d), BF16)],
        compiler_params=_cparams(("parallel", "arbitrary")),
        name="norm_proj",
    )(x, g, w)


def _rmsnorm_kernel(x_ref, g_ref, o_ref):
    o_ref[...] = _rms(x_ref[...], g_ref[...])


def rmsnorm(x, g):
    n, d = x.shape
    tm = _tile(n, ROW_TILE)
    return pl.pallas_call(
        _rmsnorm_kernel,
        grid=(n // tm,),
        in_specs=[pl.BlockSpec((tm, d), lambda i: (i, 0)), pl.BlockSpec((1, d), lambda i: (0, 0))],
        out_specs=pl.BlockSpec((tm, d), lambda i: (i, 0)),
        out_shape=jax.ShapeDtypeStruct((n, d), F32),
        compiler_params=_cparams(("parallel",)),
        name="final_norm",
    )(x, g)


def _gla_kernel(z_ref, wa_ref, ba_ref, ng_ref, s0_ref, o_ref, sout_ref,
                st_ref, cum_ref, q_ref, k_ref, v_ref, oi_ref, *, chunk, n_chunks):
    c = chunk
    hk = N_HEADS * GLA_DK
    hv = N_HEADS * GLA_DV

    @pl.when(pl.program_id(1) == 0)
    def _():
        st_ref[...] = s0_ref[0]

    tril = (_iota((c, c), 1) <= _iota((c, c), 0)).astype(F32)
    expand = (_iota((hk, hv), 0) // GLA_DK == _iota((hk, hv), 1) // GLA_DV).astype(BF16)
    state_mask = _iota((hv, hk), 0) // GLA_DV == _iota((hv, hk), 1) // GLA_DK
    group_mean = _group_ones(hv, GLA_DV) * (1.0 / GLA_DV)
    srow = _iota((c, hk), 0)

    def one_chunk(ci, carry):
        base = pl.multiple_of(ci * c, c)
        zc = z_ref[0, pl.ds(base, c), :]
        q = zc[:, 0:hk] * (GLA_DK ** -0.5)
        k = zc[:, hk:2 * hk]
        v = zc[:, 2 * hk:2 * hk + hv]
        r = zc[:, 2 * hk + hv:2 * hk + 2 * hv]
        a = zc[:, 2 * hk + 2 * hv:2 * hk + 2 * hv + LANES]
        log_a = _log_sigmoid(_dot(a, wa_ref[...], HI) + ba_ref[...]) * (1.0 / GLA_TAU)
        cum = _dot(tril, log_a, HI)
        cum_ref[...] = cum
        q_ref[...] = q
        k_ref[...] = k
        v_ref[...] = v

        def one_row(t, carry2):
            diff = cum_ref[pl.ds(t, 1), :] - cum_ref[...]
            decay = jnp.where(srow <= t, jnp.exp(diff), 0.0)
            prod = (q_ref[pl.ds(t, 1), :] * k_ref[...]) * decay
            attn = _dot(prod.astype(BF16), expand)
            oi_ref[pl.ds(t, 1), :] = jnp.sum(attn * v_ref[...], axis=0, keepdims=True)
            return carry2

        lax.fori_loop(0, c, one_row, 0, unroll=min(c, 8))

        st = st_ref[...]
        o = oi_ref[...] + _dot_nt((q * jnp.exp(cum)).astype(BF16), st.astype(BF16))
        last = cum[c - 1:c, :]
        kd = k * jnp.exp(last - cum)
        upd = _dot_tn(v.astype(BF16), kd.astype(BF16))
        st_ref[...] = st * jnp.exp(last) + jnp.where(state_mask, upd, 0.0)

        ms = _dot(o * o, group_mean, HI)
        y = o * lax.rsqrt(ms + NORM_EPS) * ng_ref[...] * (r * _sigmoid(r))
        o_ref[0, pl.ds(base, c), :] = y
        return carry

    lax.fori_loop(0, n_chunks, one_chunk, 0)

    @pl.when(pl.program_id(1) == pl.num_programs(1) - 1)
    def _():
        sout_ref[0] = st_ref[...]


def gla_mixer(z, wa, ba, ng, s0t, chunk):
    b, t, w = z.shape
    tb = _tile(t, SEQ_TILE)
    hk, hv = N_HEADS * GLA_DK, N_HEADS * GLA_DV
    kern = functools.partial(_gla_kernel, chunk=chunk, n_chunks=tb // chunk)
    return pl.pallas_call(
        kern,
        grid=(b, t // tb),
        in_specs=[pl.BlockSpec((1, tb, w), lambda i, j: (i, j, 0)),
                  pl.BlockSpec((LANES, hk), lambda i, j: (0, 0)),
                  pl.BlockSpec((1, hk), lambda i, j: (0, 0)),
                  pl.BlockSpec((1, hv), lambda i, j: (0, 0)),
                  pl.BlockSpec((1, hv, hk), lambda i, j: (i, 0, 0))],
        out_specs=[pl.BlockSpec((1, tb, hv), lambda i, j: (i, j, 0)),
                   pl.BlockSpec((1, hv, hk), lambda i, j: (i, 0, 0))],
        out_shape=[jax.ShapeDtypeStruct((b, t, hv), F32), jax.ShapeDtypeStruct((b, hv, hk), F32)],
        scratch_shapes=[pltpu.VMEM((hv, hk), F32), pltpu.VMEM((chunk, hk), F32), pltpu.VMEM((chunk, hk), F32),
                        pltpu.VMEM((chunk, hk), F32), pltpu.VMEM((chunk, hv), F32), pltpu.VMEM((chunk, hv), F32)],
        compiler_params=_cparams(("parallel", "arbitrary")),
        name="gla_mixer",
    )(z, wa, ba, ng, s0t)


def _unit_lower_inverse(m, c):
    eye = (_iota((c, c), 0) == _iota((c, c), 1)).astype(F32)
    p = -m
    inv = eye + p
    for _ in range(int(math.log2(c)) - 1):
        p = _dot(p, p, HI)
        inv = inv + _dot(inv, p, HI)
    return inv


def _gdn_kernel(z_ref, cw_ref, buf0_ref, alog_ref, dtb_ref, ng_ref, s0_ref, o_ref, sout_ref,
                s_ref, tail_ref, *, chunk, n_chunks):
    c = chunk
    w3 = 3 * GDN_W

    @pl.when(pl.program_id(1) == 0)
    def _():
        s_ref[...] = s0_ref[0]
        tail_ref[...] = buf0_ref[0]

    tril = (_iota((c, c), 1) <= _iota((c, c), 0)).astype(F32)
    triu = (_iota((c, c), 0) <= _iota((c, c), 1)).astype(F32)
    incl = _iota((c, c), 1) <= _iota((c, c), 0)
    strict = _iota((c, c), 1) < _iota((c, c), 0)
    group_ones = _group_ones(GDN_W, GDN_DH)
    row8 = _iota((SUBLANES, w3), 0)
    a_neg = -jnp.exp(alog_ref[...])

    def one_chunk(ci, carry):
        base = pl.multiple_of(ci * c, c)
        zc = z_ref[0, pl.ds(base, c), :]
        x = zc[:, 0:w3]
        gate = zc[:, w3:w3 + GDN_W]
        ab = zc[:, w3 + GDN_W:w3 + GDN_W + LANES]
        tail = tail_ref[...]
        conv = x * cw_ref[GDN_CONV - 1:GDN_CONV, :]
        for d in range(1, GDN_CONV):
            rolled = pltpu.roll(x, d, 0)
            head = jnp.where(row8 < d, pltpu.roll(tail, d, 0), rolled[0:SUBLANES, :])
            if c > SUBLANES:
                delayed = jnp.concatenate([head, rolled[SUBLANES:, :]], axis=0)
            else:
                delayed = head
            conv = conv + delayed * cw_ref[GDN_CONV - 1 - d:GDN_CONV - d, :]
        tail_ref[...] = x[c - SUBLANES:c, :]
        cv = conv * _sigmoid(conv)
        q = cv[:, 0:GDN_W]
        k = cv[:, GDN_W:2 * GDN_W]
        v = cv[:, 2 * GDN_W:3 * GDN_W]
        q = q * lax.rsqrt(_dot(q * q, group_ones, HI) + NORM_EPS) * (GDN_DH ** -0.5)
        k = k * lax.rsqrt(_dot(k * k, group_ones, HI) + NORM_EPS)
        log_alpha = a_neg * _softplus(ab + dtb_ref[...])
        log_alpha = jnp.where(_iota((c, LANES), 1) < N_HEADS, log_alpha, 0.0)
        beta = _sigmoid(ab)
        g = _dot(tril, log_alpha, HI)
        g_t = _dot_tn(log_alpha, triu, HI)
        eg = jnp.exp(g)
        outs = []
        for h in range(N_HEADS):
            sl = slice(h * GDN_DH, (h + 1) * GDN_DH)
            gh = g[:, h:h + 1]
            lmat = jnp.where(incl, jnp.exp(gh - g_t[h:h + 1, :]), 0.0)
            bh = beta[:, N_HEADS + h:N_HEADS + h + 1]
            qh, kh, vh = q[:, sl], k[:, sl], v[:, sl]
            kb = kh * bh
            m = jnp.where(strict, _dot_nt(kb, kh, HI) * lmat, 0.0)
            inv = _unit_lower_inverse(m, c)
            rhs = jnp.concatenate([vh * bh, kb * eg[:, h:h + 1]], axis=1)
            sol = _dot(inv, rhs, HI)
            u, w = sol[:, 0:GDN_DH], sol[:, GDN_DH:2 * GDN_DH]
            s = s_ref[h]
            v_new = u - _dot(w, s, HI)
            attn = _dot_nt(qh, kh, HI) * lmat
            outs.append(_dot(qh * eg[:, h:h + 1], s, HI) + _dot(attn, v_new, HI))
            g_last = g[c - 1:c, h:h + 1]
            s_ref[h] = s * jnp.exp(g_last) + _dot_tn(kh * jnp.exp(g_last - gh), v_new, HI)
        o = jnp.concatenate(outs, axis=1)
        ms = _dot(o * o, group_ones * (1.0 / GDN_DH), HI)
        o_ref[0, pl.ds(base, c), :] = o * lax.rsqrt(ms + NORM_EPS) * ng_ref[...] * (gate * _sigmoid(gate))
        return carry

    lax.fori_loop(0, n_chunks, one_chunk, 0)

    @pl.when(pl.program_id(1) == pl.num_programs(1) - 1)
    def _():
        sout_ref[0] = s_ref[...]


def gdn_mixer(z, cw, buf0, alog, dtb, ng, s0, chunk):
    b, t, w = z.shape
    tb = _tile(t, SEQ_TILE)
    kern = functools.partial(_gdn_kernel, chunk=chunk, n_chunks=tb // chunk)
    return pl.pallas_call(
        kern,
        grid=(b, t // tb),
        in_specs=[pl.BlockSpec((1, tb, w), lambda i, j: (i, j, 0)),
                  pl.BlockSpec((GDN_CONV, 3 * GDN_W), lambda i, j: (0, 0)),
                  pl.BlockSpec((1, SUBLANES, 3 * GDN_W), lambda i, j: (i, 0, 0)),
                  pl.BlockSpec((1, LANES), lambda i, j: (0, 0)),
                  pl.BlockSpec((1, LANES), lambda i, j: (0, 0)),
                  pl.BlockSpec((1, GDN_W), lambda i, j: (0, 0)),
                  pl.BlockSpec((1, N_HEADS, GDN_DH, GDN_DH), lambda i, j: (i, 0, 0, 0))],
        out_specs=[pl.BlockSpec((1, tb, GDN_W), lambda i, j: (i, j, 0)),
                   pl.BlockSpec((1, N_HEADS, GDN_DH, GDN_DH), lambda i, j: (i, 0, 0, 0))],
        out_shape=[jax.ShapeDtypeStruct((b, t, GDN_W), F32),
                   jax.ShapeDtypeStruct((b, N_HEADS, GDN_DH, GDN_DH), F32)],
        scratch_shapes=[pltpu.VMEM((N_HEADS, GDN_DH, GDN_DH), F32), pltpu.VMEM((SUBLANES, 3 * GDN_W), F32)],
        compiler_params=_cparams(("parallel", "arbitrary")),
        name="gdn_mixer",
    )(z, cw, buf0, alog, dtb, ng, s0)


def _mla_prep_kernel(z_ref, cos_ref, sin_ref, qg_ref, kvg_ref, wn_ref, wr_ref, wk_ref,
                     q_ref, kc_ref, c_ref, kr_ref):
    z = z_ref[0]
    cq = z[:, 0:MLA_Q_LORA]
    ckv = z[:, MLA_Q_LORA:MLA_Q_LORA + MLA_KV_LORA]
    kr = z[:, MLA_Q_LORA + MLA_KV_LORA:MLA_Q_LORA + MLA_KV_LORA + LANES]
    cos, sin = cos_ref[...], sin_ref[...]
    cqn = _rms(cq, qg_ref[...]).astype(BF16)
    q_nope = _dot(cqn, wn_ref[...])
    q_rope = _dot(cqn, wr_ref[...])
    q_rope = q_rope * cos + pltpu.roll(q_rope, LANES // 2, 1) * sin
    k_rope = kr * cos + pltpu.roll(kr, LANES // 2, 1) * sin
    c = _rms(ckv, kvg_ref[...])
    c_ref[0] = c
    kr_ref[0] = k_rope
    kc_ref[0] = jnp.concatenate([c, k_rope], axis=1).astype(kc_ref.dtype)
    lane_head = (_iota(q_rope.shape, 1) // (MLA_ROPE // 2)) % N_HEADS
    for h in range(N_HEADS):
        q_lat = _dot(q_nope[:, h * MLA_NOPE:(h + 1) * MLA_NOPE].astype(BF16), wk_ref[h])
        q_r = jnp.where(lane_head == h, q_rope, 0.0)
        q_ref[0, 0, h] = (jnp.concatenate([q_lat, q_r], axis=1) * MLA_SCALE).astype(q_ref.dtype)


def mla_prep(z, cos, sin, qg, kvg, wn, wr, wk, act_dtype):
    b, t, w = z.shape
    tm = _tile(t, ROW_TILE)
    dk = MLA_KV_LORA + LANES
    return pl.pallas_call(
        _mla_prep_kernel,
        grid=(b, t // tm),
        in_specs=[pl.BlockSpec((1, tm, w), lambda i, j: (i, j, 0)),
                  pl.BlockSpec((tm, LANES), lambda i, j: (j, 0)),
                  pl.BlockSpec((tm, LANES), lambda i, j: (j, 0)),
                  pl.BlockSpec((1, MLA_Q_LORA), lambda i, j: (0, 0)),
                  pl.BlockSpec((1, MLA_KV_LORA), lambda i, j: (0, 0)),
                  pl.BlockSpec(wn.shape, lambda i, j: (0, 0)),
                  pl.BlockSpec(wr.shape, lambda i, j: (0, 0)),
                  pl.BlockSpec(wk.shape, lambda i, j: (0, 0, 0))],
        out_specs=[pl.BlockSpec((1, 1, N_HEADS, tm, dk), lambda i, j: (i, 0, 0, j, 0)),
                   pl.BlockSpec((1, tm, dk), lambda i, j: (i, j, 0)),
                   pl.BlockSpec((1, tm, MLA_KV_LORA), lambda i, j: (i, j, 0)),
                   pl.BlockSpec((1, tm, LANES), lambda i, j: (i, j, 0))],
        out_shape=[jax.ShapeDtypeStruct((b, 1, N_HEADS, t, dk), act_dtype),
                   jax.ShapeDtypeStruct((b, t, dk), act_dtype),
                   jax.ShapeDtypeStruct((b, t, MLA_KV_LORA), F32),
                   jax.ShapeDtypeStruct((b, t, LANES), F32)],
        compiler_params=_cparams(("parallel", "parallel")),
        name="mla_prep",
    )(z, cos, sin, qg, kvg, wn, wr, wk)


def _split3(x):
    hi = x.astype(BF16).astype(F32)
    r1 = x - hi
    mid = r1.astype(BF16).astype(F32)
    lo = (r1 - mid).astype(BF16).astype(F32)
    return hi, mid, lo


def _fox_prep_kernel(z_ref, bf_ref, q_ref, k_ref, v_ref, lf_ref, carry_ref):
    tm = z_ref.shape[1]

    @pl.when(pl.program_id(1) == 0)
    def _():
        carry_ref[...] = jnp.zeros_like(carry_ref)

    z = z_ref[0]
    w = N_HEADS * FOX_DH
    f = z[:, 3 * w:3 * w + LANES]
    logf = jnp.where(_iota((tm, LANES), 1) < N_HEADS, _log_sigmoid(f + bf_ref[...]), 0.0)
    lf_ref[0] = logf
    tril = (_iota((tm, tm), 1) <= _iota((tm, tm), 0)).astype(F32)
    cum = _dot(tril, logf, HI) + carry_ref[...]
    carry_ref[...] = cum[tm - 1:tm, :]
    hi, mid, lo = _split3(cum)
    lane = _iota((tm, FOX_DH), 1)
    ones = jnp.where(lane < 3, 1.0, 0.0)
    for h in range(N_HEADS):
        ch = [p[:, h:h + 1] for p in (hi, mid, lo)]
        c_q = jnp.where(lane == 0, ch[0], jnp.where(lane == 1, ch[1], jnp.where(lane == 2, ch[2], 0.0)))
        c_k = jnp.where(lane == 3, -ch[0], jnp.where(lane == 4, -ch[1], jnp.where(lane == 5, -ch[2], 0.0)))
        q_tail = c_q + jnp.where((lane >= 3) & (lane < 6), 1.0, 0.0)
        k_tail = c_k + ones
        qh = z[:, h * FOX_DH:(h + 1) * FOX_DH] * FOX_SCALE
        kh = z[:, w + h * FOX_DH:w + (h + 1) * FOX_DH]
        q_ref[0, h, 0] = jnp.concatenate([qh, q_tail], axis=1).astype(BF16)
        k_ref[0, h] = jnp.concatenate([kh, k_tail], axis=1).astype(BF16)
        v_ref[0, h] = z[:, 2 * w + h * FOX_DH:2 * w + (h + 1) * FOX_DH].astype(BF16)


def fox_prep(z, bf):
    b, t, w = z.shape
    tm = _tile(t, 256)
    dk = 2 * FOX_DH
    return pl.pallas_call(
        _fox_prep_kernel,
        grid=(b, t // tm),
        in_specs=[pl.BlockSpec((1, tm, w), lambda i, j: (i, j, 0)),
                  pl.BlockSpec((1, LANES), lambda i, j: (0, 0))],
        out_specs=[pl.BlockSpec((1, N_HEADS, 1, tm, dk), lambda i, j: (i, 0, 0, j, 0)),
                   pl.BlockSpec((1, N_HEADS, tm, dk), lambda i, j: (i, 0, j, 0)),
                   pl.BlockSpec((1, N_HEADS, tm, FOX_DH), lambda i, j: (i, 0, j, 0)),
                   pl.BlockSpec((1, tm, LANES), lambda i, j: (i, j, 0))],
        out_shape=[jax.ShapeDtypeStruct((b, N_HEADS, 1, t, dk), BF16),
                   jax.ShapeDtypeStruct((b, N_HEADS, t, dk), BF16),
                   jax.ShapeDtypeStruct((b, N_HEADS, t, FOX_DH), BF16),
                   jax.ShapeDtypeStruct((b, t, LANES), F32)],
        scratch_shapes=[pltpu.VMEM((1, LANES), F32)],
        compiler_params=_cparams(("parallel", "arbitrary")),
        name="fox_prep",
    )(z, bf)


def _flash_kernel(q_ref, k_ref, v_ref, o_ref, m_ref, l_ref, acc_ref, *, hs, tq):
    qi, ki = pl.program_id(2), pl.program_id(3)
    rows = hs * tq

    @pl.when(ki == 0)
    def _():
        m_ref[...] = jnp.full_like(m_ref, -jnp.inf)
        l_ref[...] = jnp.zeros_like(l_ref)
        acc_ref[...] = jnp.zeros_like(acc_ref)

    def update(masked):
        q = q_ref[0, 0].reshape(rows, q_ref.shape[-1])
        s = _dot_nt(q, k_ref[0, 0])
        if masked:
            qpos = _iota(s.shape, 0) % tq
            s = jnp.where(qpos >= _iota(s.shape, 1), s, -jnp.inf)
        m_old = m_ref[...]
        m_new = jnp.maximum(m_old, jnp.max(s, axis=-1, keepdims=True))
        alpha = jnp.exp(m_old - m_new)
        p = jnp.exp(s - m_new)
        l_ref[...] = alpha * l_ref[...] + jnp.sum(p, axis=-1, keepdims=True)
        acc_ref[...] = alpha * acc_ref[...] + _dot(p.astype(v_ref.dtype), v_ref[0, 0])
        m_ref[...] = m_new

    @pl.when(ki < qi)
    def _():
        update(False)

    @pl.when(ki == qi)
    def _():
        update(True)
        o = acc_ref[...] / l_ref[...]
        o_ref[0, 0] = o.reshape(hs, tq, o.shape[-1])


def flash_causal(q, k, v, dv):
    b, g, hs, t, dk = q.shape
    tq = _tile(t, ATTN_TILE)
    nq = t // tq
    kern = functools.partial(_flash_kernel, hs=hs, tq=tq)
    return pl.pallas_call(
        kern,
        grid=(b, g, nq, nq),
        in_specs=[pl.BlockSpec((1, 1, hs, tq, dk), lambda bi, gi, qi, ki: (bi, gi, 0, qi, 0)),
                  pl.BlockSpec((1, 1, tq, dk), lambda bi, gi, qi, ki: (bi, gi, jnp.minimum(ki, qi), 0)),
                  pl.BlockSpec((1, 1, tq, dv), lambda bi, gi, qi, ki: (bi, gi, jnp.minimum(ki, qi), 0))],
        out_specs=pl.BlockSpec((1, 1, hs, tq, dv), lambda bi, gi, qi, ki: (bi, gi, 0, qi, 0)),
        out_shape=jax.ShapeDtypeStruct((b, g, hs, t, dv), F32),
        scratch_shapes=[pltpu.VMEM((hs * tq, 1), F32), pltpu.VMEM((hs * tq, 1), F32),
                        pltpu.VMEM((hs * tq, dv), F32)],
        compiler_params=_cparams(("parallel", "parallel", "parallel", "arbitrary")),
        name="flash_causal",
    )(q, k, v)


def _online_update(scores, values, m_ref, l_ref, acc_ref):
    s = jnp.concatenate(scores, axis=1)
    m_old = m_ref[...]
    m_new = jnp.maximum(m_old, jnp.max(s, axis=-1, keepdims=True))
    alpha = jnp.exp(m_old - m_new)
    p = jnp.exp(s - m_new)
    l_ref[...] = alpha * l_ref[...] + jnp.sum(p, axis=-1, keepdims=True)
    pv = None
    for i, v in enumerate(values):
        term = _dot(p[:, i * PAGE:(i + 1) * PAGE].astype(BF16), v)
        pv = term if pv is None else pv + term
    acc_ref[...] = alpha * acc_ref[...] + pv
    m_ref[...] = m_new


def _mla_decode_kernel(pt_ref, q_ref, kn_ref, *refs, n_new, pages):
    lat_refs, rope_refs = refs[:pages], refs[pages:2 * pages]
    o_ref, m_ref, l_ref, acc_ref = refs[2 * pages:]
    j = pl.program_id(1)
    rows = N_HEADS * n_new
    q = q_ref[0, 0].reshape(rows, q_ref.shape[-1])
    q_lat = q[:, 0:MLA_KV_LORA].astype(BF16)
    half = MLA_ROPE // 2
    sr, sc = _iota((LANES, MLA_ROPE), 0), _iota((LANES, MLA_ROPE), 1)
    compact = ((sr % half == sc % half) & ((sr >= LANES // 2) == (sc >= half))).astype(BF16)
    q_rope = _dot(q[:, MLA_KV_LORA:].astype(BF16), compact).astype(BF16)

    @pl.when(j == 0)
    def _():
        kn = kn_ref[0].astype(BF16)
        s = _dot_nt(q.astype(BF16), kn)
        s = jnp.where(_iota(s.shape, 0) % n_new >= _iota(s.shape, 1), s, -jnp.inf)
        m = jnp.max(s, axis=-1, keepdims=True)
        p = jnp.exp(s - m)
        m_ref[...] = m
        l_ref[...] = jnp.sum(p, axis=-1, keepdims=True)
        acc_ref[...] = _dot(p.astype(BF16), kn[:, 0:MLA_KV_LORA])

    scores, values = [], []
    for i in range(pages):
        c = lat_refs[i][0].astype(BF16)
        r = rope_refs[i][0].astype(BF16)
        scores.append(_dot_nt(q_lat, c) + _dot_nt(q_rope, r))
        values.append(c)
    _online_update(scores, values, m_ref, l_ref, acc_ref)

    @pl.when(j == pl.num_programs(1) - 1)
    def _():
        o = acc_ref[...] / l_ref[...]
        o_ref[0, 0] = o.reshape(N_HEADS, n_new, MLA_KV_LORA)


def mla_decode(q, kc_new, lat_pool, rope_pool, page_table, page0):
    b, _, _, n_new, dk = q.shape
    n_pages = page_table.shape[1]
    pages = _tile(n_pages, DEC_PAGES)

    def page_map(i):
        return lambda bi, j, pt: (page0 + pt[bi, j * pages + i], 0, 0)

    in_specs = [pl.BlockSpec((1, 1, N_HEADS, n_new, dk), lambda bi, j, pt: (bi, 0, 0, 0, 0)),
                pl.BlockSpec((1, n_new, dk), lambda bi, j, pt: (bi, 0, 0))]
    in_specs += [pl.BlockSpec((1, PAGE, MLA_KV_LORA), page_map(i)) for i in range(pages)]
    in_specs += [pl.BlockSpec((1, PAGE, MLA_ROPE), page_map(i)) for i in range(pages)]
    rows = N_HEADS * n_new
    kern = functools.partial(_mla_decode_kernel, n_new=n_new, pages=pages)
    return pl.pallas_call(
        kern,
        grid_spec=pltpu.PrefetchScalarGridSpec(
            num_scalar_prefetch=1,
            grid=(b, n_pages // pages),
            in_specs=in_specs,
            out_specs=pl.BlockSpec((1, 1, N_HEADS, n_new, MLA_KV_LORA), lambda bi, j, pt: (bi, 0, 0, 0, 0)),
            scratch_shapes=[pltpu.VMEM((rows, 1), F32), pltpu.VMEM((rows, 1), F32),
                            pltpu.VMEM((rows, MLA_KV_LORA), F32)]),
        out_shape=jax.ShapeDtypeStruct((b, 1, N_HEADS, n_new, MLA_KV_LORA), F32),
        compiler_params=_cparams(("parallel", "arbitrary")),
        name="mla_decode",
    )(page_table, q, kc_new, *([lat_pool] * pages), *([rope_pool] * pages))


def _fox_decode_kernel(pt_ref, z_ref, bf_ref, *refs, n_new, pages):
    k_refs, v_refs, f_refs = refs[:pages], refs[pages:2 * pages], refs[2 * pages:3 * pages]
    o_ref, lf_ref, m_ref, l_ref, acc_ref, carry_ref = refs[3 * pages:]
    j = pl.program_id(1)
    w = N_HEADS * FOX_DH
    rows = N_HEADS * n_new
    z = z_ref[0]
    q = z[:, 0:w] * FOX_SCALE
    f = z[:, 3 * w:3 * w + LANES]
    logf = jnp.where(_iota((n_new, LANES), 1) < N_HEADS, _log_sigmoid(f + bf_ref[...]), 0.0)
    tril = (_iota((n_new, n_new), 1) <= _iota((n_new, n_new), 0)).astype(F32)
    triu = (_iota((n_new, n_new), 0) <= _iota((n_new, n_new), 1)).astype(F32)
    cn = _dot(tril, logf, HI)
    cn_t = _dot_tn(logf, triu, HI)
    block = _iota((rows, w), 0) // n_new == _iota((rows, w), 1) // FOX_DH
    q_bd = jnp.where(block, jnp.concatenate([q] * N_HEADS, axis=0), 0.0).astype(BF16)
    cn_col = jnp.concatenate([cn[:, h:h + 1] for h in range(N_HEADS)], axis=0)

    @pl.when(j == 0)
    def _():
        lf_ref[0] = logf
        kn = z[:, w:2 * w].astype(BF16)
        vn = z[:, 2 * w:3 * w].astype(BF16)
        cn_row = jnp.concatenate([jnp.broadcast_to(cn_t[h:h + 1, :], (n_new, n_new)) for h in range(N_HEADS)], axis=0)
        s = _dot_nt(q_bd, kn) + cn_col - cn_row
        s = jnp.where(_iota(s.shape, 0) % n_new >= _iota(s.shape, 1), s, -jnp.inf)
        m = jnp.max(s, axis=-1, keepdims=True)
        p = jnp.exp(s - m)
        m_ref[...] = m
        l_ref[...] = jnp.sum(p, axis=-1, keepdims=True)
        acc_ref[...] = _dot(p.astype(BF16), vn)
        carry_ref[...] = jnp.zeros_like(carry_ref)

    lp = jnp.concatenate([f_refs[i][0] for i in range(pages)], axis=0)
    later = (_iota((PAGE, PAGE), 0) > _iota((PAGE, PAGE), 1)).astype(F32)
    after = _dot(lp, later, HI)
    total = jnp.sum(lp, axis=1, keepdims=True)
    carry = carry_ref[...]
    scores, values = [], []
    for i in range(pages):
        after_i = after[i * N_HEADS:(i + 1) * N_HEADS, :] + carry
        carry = carry + total[i * N_HEADS:(i + 1) * N_HEADS, :]
        bias = jnp.concatenate([jnp.broadcast_to(after_i[h:h + 1, :], (n_new, PAGE)) for h in range(N_HEADS)],
                               axis=0)
        scores.append(_dot_nt(q_bd, k_refs[i][0].astype(BF16)) + cn_col + bias)
        values.append(v_refs[i][0].astype(BF16))
    carry_ref[...] = carry
    _online_update(scores, values, m_ref, l_ref, acc_ref)

    @pl.when(j == pl.num_programs(1) - 1)
    def _():
        o = jnp.where(block, acc_ref[...] / l_ref[...], 0.0)
        out = o[0:n_new, :]
        for h in range(1, N_HEADS):
            out = out + o[h * n_new:(h + 1) * n_new, :]
        o_ref[0] = out


def fox_decode(z, bf, k_pool, v_pool, f_pool, page_table, page0):
    b, n_new, zw = z.shape
    n_pages = page_table.shape[1]
    pages = _tile(n_pages, DEC_PAGES)
    w = N_HEADS * FOX_DH

    def page_map(i):
        return lambda bi, j, pt: (page0 + pt[bi, n_pages - 1 - (j * pages + i)], 0, 0)

    in_specs = [pl.BlockSpec((1, n_new, zw), lambda bi, j, pt: (bi, 0, 0)),
                pl.BlockSpec((1, LANES), lambda bi, j, pt: (0, 0))]
    in_specs += [pl.BlockSpec((1, PAGE, w), page_map(i)) for i in range(pages)]
    in_specs += [pl.BlockSpec((1, PAGE, w), page_map(i)) for i in range(pages)]
    in_specs += [pl.BlockSpec((1, N_HEADS, PAGE), page_map(i)) for i in range(pages)]
    rows = N_HEADS * n_new
    kern = functools.partial(_fox_decode_kernel, n_new=n_new, pages=pages)
    return pl.pallas_call(
        kern,
        grid_spec=pltpu.PrefetchScalarGridSpec(
            num_scalar_prefetch=1,
            grid=(b, n_pages // pages),
            in_specs=in_specs,
            out_specs=[pl.BlockSpec((1, n_new, w), lambda bi, j, pt: (bi, 0, 0)),
                       pl.BlockSpec((1, n_new, LANES), lambda bi, j, pt: (bi, 0, 0))],
            scratch_shapes=[pltpu.VMEM((rows, 1), F32), pltpu.VMEM((rows, 1), F32),
                            pltpu.VMEM((rows, w), F32), pltpu.VMEM((N_HEADS, 1), F32)]),
        out_shape=[jax.ShapeDtypeStruct((b, n_new, w), F32), jax.ShapeDtypeStruct((b, n_new, LANES), F32)],
        compiler_params=_cparams(("parallel", "arbitrary")),
        name="fox_decode",
    )(page_table, z, bf, *([k_pool] * pages), *([v_pool] * pages), *([f_pool] * pages))


def _merge_kernel(x_ref, oa_ref, olat_ref, oc_ref, od_ref, gate_ref, wuv_ref, wb_ref, wo_ref, h_ref):
    d = x_ref.shape[1]
    olat = olat_ref[...]
    ob = jnp.concatenate([_dot(olat[:, h * MLA_KV_LORA:(h + 1) * MLA_KV_LORA].astype(BF16), wuv_ref[h])
                          for h in range(N_HEADS)], axis=1)
    acc = None
    for n, o in enumerate((oa_ref[...], ob, oc_ref[...], od_ref[...])):
        term = _sigmoid(gate_ref[:, n * d:(n + 1) * d]) * _dot(o.astype(BF16), wb_ref[n])
        acc = term if acc is None else acc + term
    h_ref[...] = x_ref[...] + _dot(acc.astype(BF16), wo_ref[...])


def merge(x, oa, olat, oc, od, gate, wuv, wb, wo):
    n, d = x.shape
    tm = _tile(n, ROW_TILE)
    row = lambda width: pl.BlockSpec((tm, width), lambda i: (i, 0))
    return pl.pallas_call(
        _merge_kernel,
        grid=(n // tm,),
        in_specs=[row(d), row(BRANCH_W), row(N_HEADS * MLA_KV_LORA), row(BRANCH_W), row(BRANCH_W), row(4 * d),
                  pl.BlockSpec(wuv.shape, lambda i: (0, 0, 0)),
                  pl.BlockSpec(wb.shape, lambda i: (0, 0, 0)),
                  pl.BlockSpec(wo.shape, lambda i: (0, 0))],
        out_specs=row(d),
        out_shape=jax.ShapeDtypeStruct((n, d), F32),
        compiler_params=_cparams(("parallel",)),
        name="merge",
    )(x, oa, olat, oc, od, gate, wuv, wb, wo)


def _peer_select_kernel(h_ref, g_ref, wq_ref, sk_ref, xn_ref, idx_ref, gate_ref, q_ref, sc_ref):
    hd = pl.program_id(1)
    tm = h_ref.shape[0]
    qd = 2 * PEER_HALF
    k2 = PEER_TOPK * PEER_TOPK

    @pl.when(hd == 0)
    def _():
        xn = _rms(h_ref[...], g_ref[...])
        xn_ref[...] = xn
        q = _dot(xn, wq_ref[...], HI)
        for i in range(PEER_HEADS):
            q_ref[i] = q[:, i * qd:(i + 1) * qd]
        idx_ref[...] = jnp.zeros_like(idx_ref)
        sc_ref[...] = jnp.zeros_like(sc_ref)

    qh = q_ref[hd]
    lane = _iota((tm, PEER_NKEYS), 1)
    lane2 = _iota((tm, k2), 1)
    sub0 = _dot_nt(qh[:, 0:PEER_HALF], sk_ref[0, 0], HI)
    sub1 = _dot_nt(qh[:, PEER_HALF:2 * PEER_HALF], sk_ref[0, 1], HI)

    def pick(a, carry):
        w0, w1, vals, ids = carry
        m0 = jnp.max(w0, axis=-1, keepdims=True)
        m1 = jnp.max(w1, axis=-1, keepdims=True)
        a0 = jnp.argmax(w0, axis=-1, keepdims=True).astype(I32)
        a1 = jnp.argmax(w1, axis=-1, keepdims=True).astype(I32)
        vals = jnp.where(lane == a, m0, jnp.where(lane == PEER_TOPK + a, m1, vals))
        ids = jnp.where(lane == a, a0.astype(F32), jnp.where(lane == PEER_TOPK + a, a1.astype(F32), ids))
        return jnp.where(lane == a0, -jnp.inf, w0), jnp.where(lane == a1, -jnp.inf, w1), vals, ids

    zeros = jnp.zeros((tm, PEER_NKEYS), F32)
    _, _, vals, ids = lax.fori_loop(0, PEER_TOPK, pick, (sub0, sub1, zeros, zeros))

    er, ec = _iota((PEER_NKEYS, k2), 0), _iota((PEER_NKEYS, k2), 1)
    spread0 = (er == ec // PEER_TOPK).astype(BF16)
    spread1 = (er - PEER_TOPK == ec % PEER_TOPK).astype(BF16)

    def spread(x, e):
        hi, mid, lo = _split3(x)
        return (_dot(hi.astype(BF16), e) + _dot(mid.astype(BF16), e)) + _dot(lo.astype(BF16), e)

    cand_s = spread(vals, spread0) + spread(vals, spread1)
    idb = ids.astype(BF16)
    cand_i = _dot(idb, spread0) * float(PEER_NKEYS) + _dot(idb, spread1)

    ci_lo, ci_hi = cand_i[:, 0:PEER_NKEYS], cand_i[:, PEER_NKEYS:k2]

    def best(kk, carry):
        w_lo, w_hi, sc, ix = carry
        m_lo = jnp.max(w_lo, axis=-1, keepdims=True)
        m_hi = jnp.max(w_hi, axis=-1, keepdims=True)
        a_lo = jnp.argmax(w_lo, axis=-1, keepdims=True).astype(I32)
        a_hi = jnp.argmax(w_hi, axis=-1, keepdims=True).astype(I32)
        use_hi = m_hi > m_lo
        hit_lo = jnp.logical_and(lane == a_lo, jnp.logical_not(use_hi))
        hit_hi = jnp.logical_and(lane == a_hi, use_hi)
        picked = jnp.where(hit_lo, ci_lo, jnp.where(hit_hi, ci_hi, -1.0))
        expert = jnp.max(picked, axis=-1, keepdims=True).astype(I32)
        out = lane == hd * PEER_TOPK + kk
        return (jnp.where(hit_lo, -jnp.inf, w_lo), jnp.where(hit_hi, -jnp.inf, w_hi),
                jnp.where(out, jnp.maximum(m_lo, m_hi), sc), jnp.where(out, expert, ix))

    _, _, sc, ix = lax.fori_loop(0, PEER_TOPK, best,
                                 (cand_s[:, 0:PEER_NKEYS], cand_s[:, PEER_NKEYS:k2], sc_ref[...], idx_ref[...]))
    sc_ref[...] = sc
    idx_ref[...] = ix

    @pl.when(hd == pl.num_programs(1) - 1)
    def _():
        grp = lane // PEER_TOPK
        mx = jnp.zeros((tm, PEER_NKEYS), F32)
        for i in range(PEER_HEADS):
            mx = jnp.where(grp == i, jnp.max(jnp.where(grp == i, sc, -jnp.inf), axis=-1, keepdims=True), mx)
        e = jnp.exp(sc - mx)
        den = _dot(e, _group_ones(PEER_NKEYS, PEER_TOPK), HI)
        gate_ref[...] = e / den


def peer_select(h, g, wq, sk):
    n, d = h.shape
    tm = _tile(n, SELECT_TILE)
    nk = PEER_HEADS * PEER_TOPK
    return pl.pallas_call(
        _peer_select_kernel,
        grid=(n // tm, PEER_HEADS),
        in_specs=[pl.BlockSpec((tm, d), lambda i, j: (i, 0)),
                  pl.BlockSpec((1, d), lambda i, j: (0, 0)),
                  pl.BlockSpec(wq.shape, lambda i, j: (0, 0)),
                  pl.BlockSpec((1, 2, PEER_NKEYS, PEER_HALF), lambda i, j: (j, 0, 0, 0))],
        out_specs=[pl.BlockSpec((tm, d), lambda i, j: (i, 0)),
                   pl.BlockSpec((tm, nk), lambda i, j: (i, 0)),
                   pl.BlockSpec((tm, nk), lambda i, j: (i, 0))],
        out_shape=[jax.ShapeDtypeStruct((n, d), F32), jax.ShapeDtypeStruct((n, nk), I32),
                   jax.ShapeDtypeStruct((n, nk), F32)],
        scratch_shapes=[pltpu.VMEM((PEER_HEADS, tm, 2 * PEER_HALF), F32), pltpu.VMEM((tm, nk), F32)],
        compiler_params=_cparams(("parallel", "arbitrary")),
        name="peer_select",
    )(h, g, wq, sk)


def _gelu(x):
    return 0.5 * x * (1.0 + lax.erf(x * (2.0 ** -0.5)))


def _peer_gather_kernel(idx_ref, idxn_ref, gate_ref, xn_ref, h_ref, uv_hbm, y_ref, buf, sem, wb_ref, *, tokens):
    i = pl.program_id(0)
    n = pl.num_programs(0)
    nk = PEER_HEADS * PEER_TOPK
    picks = tokens * nk

    rpe = 2 * SUBLANES

    def issue(ids_ref, slot):
        for r in range(picks):
            pltpu.make_async_copy(uv_hbm.at[ids_ref[r]], buf.at[slot, pl.ds(r * rpe, rpe), :], sem.at[slot]).start()

    def compute(slot):
        pltpu.make_async_copy(buf.at[slot], buf.at[slot], sem.at[slot]).wait()
        gate_t = jnp.transpose(gate_ref[...])
        for t in range(tokens):
            base = t * nk * rpe
            u = jnp.concatenate([buf[slot, pl.ds(base + s, nk, stride=rpe), :] for s in range(SUBLANES)],
                                axis=1)
            x8 = jnp.broadcast_to(xn_ref[t:t + 1, :], (SUBLANES, u.shape[1]))
            act = _dot_nt(u, x8)[:, 0:1]
            wgt = gate_t[:, t:t + 1] * _gelu(act)
            wb_ref[...] = jnp.broadcast_to(wgt, (nk, LANES))
            parts = [None] * 4
            for jj in range(nk):
                term = wb_ref[jj:jj + 1, :] * buf[slot, pl.ds(base + jj * rpe + SUBLANES, SUBLANES), :]
                parts[jj % 4] = term if parts[jj % 4] is None else parts[jj % 4] + term
            y_ref[t] = h_ref[t] + ((parts[0] + parts[1]) + (parts[2] + parts[3]))

    @pl.when(i == 0)
    def _():
        issue(idx_ref, 0)

    for s in range(2):
        @pl.when(jnp.logical_and(i + 1 < n, (i + 1) % 2 == s))
        def _(s=s):
            issue(idxn_ref, s)

    for s in range(2):
        @pl.when(i % 2 == s)
        def _(s=s):
            compute(s)


def peer_gather(idx, gate, xn, h, uv):
    n, d = h.shape
    tb = _tile(n, GATHER_TOKENS)
    nk = PEER_HEADS * PEER_TOPK
    steps = n // tb
    h3 = h.reshape(n, SUBLANES, d // SUBLANES)
    tile3 = pl.BlockSpec((tb, SUBLANES, d // SUBLANES), lambda i: (i, 0, 0))
    kern = functools.partial(_peer_gather_kernel, tokens=tb)
    y = pl.pallas_call(
        kern,
        grid=(steps,),
        in_specs=[pl.BlockSpec((tb * nk,), lambda i: (i,), memory_space=pltpu.SMEM),
                  pl.BlockSpec((tb * nk,), lambda i: (jnp.minimum(i + 1, steps - 1),), memory_space=pltpu.SMEM),
                  pl.BlockSpec((tb, nk), lambda i: (i, 0)),
                  pl.BlockSpec((tb, d), lambda i: (i, 0)),
                  tile3,
                  pl.BlockSpec(memory_space=pl.ANY)],
        out_specs=tile3,
        out_shape=jax.ShapeDtypeStruct(h3.shape, F32),
        scratch_shapes=[pltpu.VMEM((2, tb * nk * 2 * SUBLANES, d // SUBLANES), F32),
                        pltpu.SemaphoreType.DMA((2,)),
                        pltpu.VMEM((nk, nk), F32)],
        compiler_params=_cparams(("arbitrary",)),
        name="peer_gather",
    )(idx.reshape(n * nk), idx.reshape(n * nk), gate, xn, h3, uv)
    return y.reshape(n, d)


def _pad_cols(w, width):
    return jnp.pad(w, ((0, 0), (0, width - w.shape[1])))


def _layer_weights(l, p):
    d = p['w_in'].shape[1]
    w_in = p['w_in'][l]
    hk, hv = N_HEADS * GLA_DK, N_HEADS * GLA_DV
    off = 0
    cols = {}
    for name, width in (('gla_q', hk), ('gla_k', hk), ('gla_v', hv), ('gla_r', hv), ('gla_a', GLA_LR),
                        ('mla_cq', MLA_Q_LORA), ('mla_ckv', MLA_KV_LORA), ('mla_kr', MLA_ROPE),
                        ('fox_q', GDN_W), ('fox_k', GDN_W), ('fox_v', GDN_W), ('fox_f', N_HEADS),
                        ('gdn_qkv', 3 * GDN_W), ('gdn_a', N_HEADS), ('gdn_b', N_HEADS), ('gdn_g', GDN_W),
                        ('gate', 4 * d)):
        cols[name] = w_in[:, off:off + width]
        off += width
    assert off == w_in.shape[1]
    half = MLA_ROPE // 2
    kr = cols['mla_kr']
    kr_tiled = jnp.concatenate([jnp.tile(kr[:, :half], (1, N_HEADS)), jnp.tile(kr[:, half:], (1, N_HEADS))], axis=1)
    lw = {
        'w_gla': jnp.concatenate([cols['gla_q'], cols['gla_k'], cols['gla_v'], cols['gla_r'],
                                  _pad_cols(cols['gla_a'], LANES)], axis=1).astype(BF16),
        'w_mla': jnp.concatenate([cols['mla_cq'], cols['mla_ckv'], kr_tiled], axis=1).astype(BF16),
        'w_fox': jnp.concatenate([cols['fox_q'], cols['fox_k'], cols['fox_v'],
                                  _pad_cols(cols['fox_f'], LANES)], axis=1).astype(BF16),
        'w_gdn': jnp.concatenate([cols['gdn_qkv'], cols['gdn_g'],
                                  _pad_cols(jnp.concatenate([cols['gdn_a'], cols['gdn_b']], axis=1), LANES)],
                                 axis=1).astype(BF16),
        'w_gate': cols['gate'].astype(BF16),
    }
    lw['gla_wa'] = jnp.pad(p['gla_w_a2'][l], ((0, LANES - GLA_LR), (0, 0)))
    lw['gla_ba'] = p['gla_b_a'][l][None, :]
    lw['gla_ng'] = jnp.tile(p['gla_norm_g'][l], N_HEADS)[None, :]
    w_uq = p['mla_w_uq'][l]
    lw['mla_wn'] = w_uq[:, :, :MLA_NOPE].reshape(MLA_Q_LORA, N_HEADS * MLA_NOPE).astype(BF16)
    lw['mla_wr'] = jnp.concatenate([w_uq[:, :, MLA_NOPE:MLA_NOPE + half].reshape(MLA_Q_LORA, N_HEADS * half),
                                    w_uq[:, :, MLA_NOPE + half:].reshape(MLA_Q_LORA, N_HEADS * half)],
                                   axis=1).astype(BF16)
    lw['mla_wk'] = jnp.transpose(p['mla_w_uk'][l], (1, 2, 0)).astype(BF16)
    lw['mla_wuv'] = jnp.transpose(p['mla_w_uv'][l], (1, 0, 2)).astype(BF16)
    lw['mla_qg'] = p['mla_qnorm_g'][l][None, :]
    lw['mla_kvg'] = p['mla_kvnorm_g'][l][None, :]
    lw['fox_bf'] = _pad_cols(p['fox_b_f'][l][None, :], LANES)
    lw['gdn_cw'] = p['gdn_conv_w'][l]
    lw['gdn_alog'] = _pad_cols(p['gdn_a_log'][l][None, :], LANES)
    lw['gdn_dtb'] = _pad_cols(p['gdn_dt_bias'][l][None, :], LANES)
    lw['gdn_ng'] = jnp.tile(p['gdn_norm_g'][l], N_HEADS)[None, :]
    lw['wb'] = p['w_branch'][l].astype(BF16)
    lw['wo'] = p['w_out'][l].astype(BF16)
    lw['norm1_g'] = p['norm1_g'][l][None, :]
    lw['norm2_g'] = p['norm2_g'][l][None, :]
    lw['peer_wq'] = p['peer_w_q'][l]
    lw['peer_sk'] = p['peer_subkeys'][l]
    n_exp = p['peer_u'].shape[1]
    lw['peer_uv'] = jnp.concatenate([p['peer_u'][l].reshape(n_exp, SUBLANES, d // SUBLANES),
                                     p['peer_v'][l].reshape(n_exp, SUBLANES, d // SUBLANES)], axis=1)
    return lw


def _rope_tables(pos):
    half = MLA_ROPE // 2
    inv = ROPE_BASE ** (-jnp.arange(half, dtype=F32) / half)
    ang = pos.astype(F32)[:, None] * inv[None, :]
    cos, sin = jnp.cos(ang), jnp.sin(ang)
    cos_t = jnp.tile(cos, (1, 2 * N_HEADS))
    sin_t = jnp.concatenate([jnp.tile(-sin, (1, N_HEADS)), jnp.tile(sin, (1, N_HEADS))], axis=1)
    return cos_t, sin_t


def _gla_state_to_kernel(s):
    b = s.shape[0]
    eye = jnp.eye(N_HEADS, dtype=s.dtype)
    return jnp.einsum('bhdv,hg->bhvgd', s, eye).reshape(b, N_HEADS * GLA_DV, N_HEADS * GLA_DK)


def _gla_state_from_kernel(st):
    b = st.shape[0]
    s5 = st.reshape(b, N_HEADS, GLA_DV, N_HEADS, GLA_DK)
    diag = jnp.stack([s5[:, h, :, h, :] for h in range(N_HEADS)], axis=1)
    return jnp.swapaxes(diag, 2, 3)


def _token_mix(x, lw, pos, past, l):
    b, t, d = x.shape
    n = b * t
    x2 = x.reshape(n, d)
    g1 = lw['norm1_g']
    z_gla = norm_proj(x2, g1, lw['w_gla']).reshape(b, t, -1)
    z_mla = norm_proj(x2, g1, lw['w_mla']).reshape(b, t, -1)
    z_fox = norm_proj(x2, g1, lw['w_fox']).reshape(b, t, -1)
    z_gdn = norm_proj(x2, g1, lw['w_gdn']).reshape(b, t, -1)
    z_gate = norm_proj(x2, g1, lw['w_gate'])

    if past is None:
        gla_s0 = jnp.zeros((b, N_HEADS * GLA_DV, N_HEADS * GLA_DK), F32)
    else:
        gla_s0 = _gla_state_to_kernel(past['state_gla'][l])
    o_a, gla_st = gla_mixer(z_gla, lw['gla_wa'], lw['gla_ba'], lw['gla_ng'], gla_s0, min(GLA_CHUNK, t))
    gla_s = _gla_state_from_kernel(gla_st)

    assert t >= GDN_CONV - 1
    if past is None:
        gdn_s0 = jnp.zeros((b, N_HEADS, GDN_DH, GDN_DH), F32)
        buf0 = jnp.zeros((b, SUBLANES, 3 * GDN_W), F32)
    else:
        gdn_s0 = past['state_gdn'][l]
        buf0 = jnp.pad(past['state_gdn_conv'][l], ((0, 0), (SUBLANES - (GDN_CONV - 1), 0), (0, 0)))
    o_d, gdn_s = gdn_mixer(z_gdn, lw['gdn_cw'], buf0, lw['gdn_alog'], lw['gdn_dtb'], lw['gdn_ng'], gdn_s0,
                           min(GDN_CHUNK, t))
    gdn_buf = z_gdn[:, t - (GDN_CONV - 1):, :3 * GDN_W]

    cos_t, sin_t = _rope_tables(pos)
    act_dtype = BF16 if past is None else F32
    q_m, kc, c_lat, kr_t = mla_prep(z_mla, cos_t, sin_t, lw['mla_qg'], lw['mla_kvg'], lw['mla_wn'], lw['mla_wr'],
                                    lw['mla_wk'], act_dtype)
    half = MLA_ROPE // 2
    k_rope = jnp.concatenate([kr_t[..., :half], kr_t[..., LANES // 2:LANES // 2 + half]], axis=-1)
    if past is None:
        o_lat = flash_causal(q_m, kc[:, None], kc[:, None], MLA_KV_LORA)
    else:
        o_lat = mla_decode(q_m, kc, past['lat_pool'], past['rope_pool'], past['page_table'], l * past['n_pool'])
    o_lat = jnp.transpose(o_lat[:, 0], (0, 2, 1, 3)).reshape(n, N_HEADS * MLA_KV_LORA)

    w = N_HEADS * FOX_DH
    fk = z_fox[..., w:2 * w].reshape(b, t, N_HEADS, FOX_DH)
    fv = z_fox[..., 2 * w:3 * w].reshape(b, t, N_HEADS, FOX_DH)
    if past is None:
        q_f, k_f, v_f, lf = fox_prep(z_fox, lw['fox_bf'])
        o_c = flash_causal(q_f, k_f, v_f, FOX_DH)
        o_c = jnp.transpose(o_c[:, :, 0], (0, 2, 1, 3)).reshape(n, w)
    else:
        o_c, lf = fox_decode(z_fox, lw['fox_bf'], past['k_pool'], past['v_pool'], past['f_pool'],
                             past['page_table'], l * past['n_pool'])
        o_c = o_c.reshape(n, w)
    logf = lf[..., :N_HEADS]

    h = merge(x2, o_a.reshape(n, -1), o_lat, o_c, o_d.reshape(n, -1), z_gate, lw['mla_wuv'], lw['wb'], lw['wo'])
    state = {'mla_latent': c_lat, 'mla_rope': k_rope, 'fox_k': fk, 'fox_v': fv, 'fox_logf': logf,
             'gla': gla_s, 'gdn': gdn_s, 'gdn_conv': gdn_buf}
    return h, state


def _run_trunk(x, layer_weights, final_g, pos, past):
    b, t, d = x.shape
    states = []
    for l, lw in enumerate(layer_weights):
        h, st = _token_mix(x, lw, pos, past, l)
        xn, idx, gate = peer_select(h, lw['norm2_g'], lw['peer_wq'], lw['peer_sk'])
        x = peer_gather(idx, gate, xn, h, lw['peer_uv']).reshape(b, t, d)
        states.append(st)
    stacked = {name: jnp.stack([st[name] for st in states]) for name in states[0]}
    y = rmsnorm(x.reshape(b * t, d), final_g[None, :]).reshape(b, t, d)
    return y, stacked


def kernel(x_prompt, x_sample, cache_mla_latent, cache_mla_rope, cache_fox_k, cache_fox_v, cache_fox_logf,
           state_gla, state_gdn, state_gdn_conv, page_table,
           norm1_g, w_in, gla_w_a2, gla_b_a, gla_norm_g, mla_qnorm_g, mla_w_uq, mla_kvnorm_g, mla_w_uk, mla_w_uv,
           fox_b_f, gdn_conv_w, gdn_a_log, gdn_dt_bias, gdn_norm_g, w_branch, w_out, norm2_g,
           peer_w_q, peer_subkeys, peer_u, peer_v, final_norm_g):
    params = {
        'norm1_g': norm1_g, 'w_in': w_in, 'gla_w_a2': gla_w_a2, 'gla_b_a': gla_b_a, 'gla_norm_g': gla_norm_g,
        'mla_qnorm_g': mla_qnorm_g, 'mla_w_uq': mla_w_uq, 'mla_kvnorm_g': mla_kvnorm_g, 'mla_w_uk': mla_w_uk,
        'mla_w_uv': mla_w_uv, 'fox_b_f': fox_b_f, 'gdn_conv_w': gdn_conv_w, 'gdn_a_log': gdn_a_log,
        'gdn_dt_bias': gdn_dt_bias, 'gdn_norm_g': gdn_norm_g, 'w_branch': w_branch, 'w_out': w_out,
        'norm2_g': norm2_g, 'peer_w_q': peer_w_q, 'peer_subkeys': peer_subkeys, 'peer_u': peer_u, 'peer_v': peer_v,
    }
    depth, n_pool = cache_mla_latent.shape[:2]
    fox_w = N_HEADS * FOX_DH
    past = {
        'n_pool': n_pool,
        'lat_pool': cache_mla_latent.reshape(depth * n_pool, PAGE, MLA_KV_LORA),
        'rope_pool': cache_mla_rope.reshape(depth * n_pool, PAGE, MLA_ROPE),
        'k_pool': cache_fox_k.reshape(depth * n_pool, PAGE, fox_w),
        'v_pool': cache_fox_v.reshape(depth * n_pool, PAGE, fox_w),
        'f_pool': jnp.swapaxes(cache_fox_logf, 2, 3).reshape(depth * n_pool, N_HEADS, PAGE),
        'state_gla': state_gla, 'state_gdn': state_gdn, 'state_gdn_conv': state_gdn_conv, 'page_table': page_table,
    }
    layer_weights = [_layer_weights(l, params) for l in range(depth)]
    past_len = page_table.shape[1] * cache_mla_latent.shape[2]
    pos_prompt = jnp.arange(x_prompt.shape[1], dtype=I32)
    pos_sample = past_len + jnp.arange(x_sample.shape[1], dtype=I32)
    y_p, st_p = _run_trunk(x_prompt, layer_weights, final_norm_g, pos_prompt, None)
    y_s, st_s = _run_trunk(x_sample, layer_weights, final_norm_g, pos_sample, past)
    return (y_p, y_s,
            st_p['mla_latent'], st_s['mla_latent'], st_p['mla_rope'], st_s['mla_rope'],
            st_p['fox_k'], st_s['fox_k'], st_p['fox_v'], st_s['fox_v'],
            st_p['fox_logf'], st_s['fox_logf'], st_p['gla'], st_s['gla'],
            st_p['gdn'], st_s['gdn'], st_p['gdn_conv'], st_s['gdn_conv'])
```

```python
import functools
import math

import jax
import jax.numpy as jnp
from jax import lax
from jax.experimental import pallas as pl
from jax.experimental.pallas import tpu as pltpu

F32 = jnp.float32
BF16 = jnp.bfloat16
I32 = jnp.int32
HI = lax.Precision.HIGHEST

NORM_EPS = 1e-6
N_HEADS = 4
GLA_DK, GLA_DV, GLA_LR, GLA_TAU, GLA_CHUNK = 32, 64, 16, 16.0, 64
MLA_NOPE, MLA_ROPE, MLA_KV_LORA, MLA_Q_LORA, MLA_V = 64, 32, 128, 256, 64
MLA_SCALE = (MLA_NOPE + MLA_ROPE) ** -0.5
ROPE_BASE = 10000.0
FOX_DH = 64
FOX_SCALE = FOX_DH ** -0.5
GDN_DH, GDN_CONV, GDN_CHUNK = 64, 4, 64
GDN_W = N_HEADS * GDN_DH
PEER_HEADS, PEER_NKEYS, PEER_HALF, PEER_TOPK = 8, 128, 64, 16
BRANCH_W = 256
PAGE = 128

LANES = 128
SUBLANES = 8
VMEM_LIMIT = 56 * 1024 * 1024

ROW_TILE = 512
SEQ_TILE = 512
ATTN_TILE = 1024
SELECT_TILE = 256
GATHER_TOKENS = 8
DEC_PAGES = 16


def _tile(n, pref):
    t = min(n, pref)
    assert n % t == 0, (n, pref)
    return t


def _cparams(sem):
    return pltpu.CompilerParams(dimension_semantics=sem, vmem_limit_bytes=VMEM_LIMIT)


def _rms(x, g):
    return x * lax.rsqrt(jnp.mean(x * x, axis=-1, keepdims=True) + NORM_EPS) * g


def _softplus(x):
    return jnp.maximum(x, 0.0) + jnp.log1p(jnp.exp(-jnp.abs(x)))


def _log_sigmoid(x):
    return -_softplus(-x)


def _sigmoid(x):
    return 1.0 / (1.0 + jnp.exp(-x))


def _iota(shape, axis):
    return lax.broadcasted_iota(I32, shape, axis)


def _dot(a, b, precision=None):
    return jnp.dot(a, b, precision=precision, preferred_element_type=F32)


def _dot_nt(a, b, precision=None):
    return lax.dot_general(a, b, (((1,), (1,)), ((), ())), precision=precision, preferred_element_type=F32)


def _dot_tn(a, b, precision=None):
    return lax.dot_general(a, b, (((0,), (0,)), ((), ())), precision=precision, preferred_element_type=F32)


def _group_ones(n, group, dtype=F32):
    return (_iota((n, n), 0) // group == _iota((n, n), 1) // group).astype(dtype)


def _norm_proj_kernel(x_ref, g_ref, w_ref, o_ref, xn_ref):
    @pl.when(pl.program_id(1) == 0)
    def _():
        xn_ref[...] = _rms(x_ref[...], g_ref[...]).astype(BF16)

    o_ref[...] = _dot(xn_ref[...], w_ref[...])


def norm_proj(x, g, w):
    n, d = x.shape
    m = w.shape[1]
    tm = _tile(n, ROW_TILE)
    tn = m if m <= 1280 else _tile(m, 1024)
    return pl.pallas_call(
        _norm_proj_kernel,
        grid=(n // tm, m // tn),
        in_specs=[pl.BlockSpec((tm, d), lambda i, j: (i, 0)),
                  pl.BlockSpec((1, d), lambda i, j: (0, 0)),
                  pl.BlockSpec((d, tn), lambda i, j: (0, j))],
        out_specs=pl.BlockSpec((tm, tn), lambda i, j: (i, j)),
        out_shape=jax.ShapeDtypeStruct((n, m), F32),
        scratch_shapes=[pltpu.VMEM((tm, d), BF16)],
        compiler_params=_cparams(("parallel", "arbitrary")),
        name="norm_proj",
    )(x, g, w)


def _rmsnorm_kernel(x_ref, g_ref, o_ref):
    o_ref[...] = _rms(x_ref[...], g_ref[...])


def rmsnorm(x, g):
    n, d = x.shape
    tm = _tile(n, ROW_TILE)
    return pl.pallas_call(
        _rmsnorm_kernel,
        grid=(n // tm,),
        in_specs=[pl.BlockSpec((tm, d), lambda i: (i, 0)), pl.BlockSpec((1, d), lambda i: (0, 0))],
        out_specs=pl.BlockSpec((tm, d), lambda i: (i, 0)),
        out_shape=jax.ShapeDtypeStruct((n, d), F32),
        compiler_params=_cparams(("parallel",)),
        name="final_norm",
    )(x, g)


def _gla_kernel(z_ref, wa_ref, ba_ref, ng_ref, s0_ref, o_ref, sout_ref,
                st_ref, cum_ref, q_ref, k_ref, v_ref, oi_ref, *, chunk, n_chunks):
    c = chunk
    hk = N_HEADS * GLA_DK
    hv = N_HEADS * GLA_DV

    @pl.when(pl.program_id(1) == 0)
    def _():
        st_ref[...] = s0_ref[0]

    tril = (_iota((c, c), 1) <= _iota((c, c), 0)).astype(F32)
    expand = (_iota((hk, hv), 0) // GLA_DK == _iota((hk, hv), 1) // GLA_DV).astype(BF16)
    state_mask = _iota((hv, hk), 0) // GLA_DV == _iota((hv, hk), 1) // GLA_DK
    group_mean = _group_ones(hv, GLA_DV) * (1.0 / GLA_DV)
    srow = _iota((c, hk), 0)

    def one_chunk(ci, carry):
        base = pl.multiple_of(ci * c, c)
        zc = z_ref[0, pl.ds(base, c), :]
        q = zc[:, 0:hk] * (GLA_DK ** -0.5)
        k = zc[:, hk:2 * hk]
        v = zc[:, 2 * hk:2 * hk + hv]
        r = zc[:, 2 * hk + hv:2 * hk + 2 * hv]
        a = zc[:, 2 * hk + 2 * hv:2 * hk + 2 * hv + LANES]
        log_a = _log_sigmoid(_dot(a, wa_ref[...], HI) + ba_ref[...]) * (1.0 / GLA_TAU)
        cum = _dot(tril, log_a, HI)
        cum_ref[...] = cum
        q_ref[...] = q
        k_ref[...] = k
        v_ref[...] = v

        def one_row(t, carry2):
            diff = cum_ref[pl.ds(t, 1), :] - cum_ref[...]
            decay = jnp.where(srow <= t, jnp.exp(diff), 0.0)
            prod = (q_ref[pl.ds(t, 1), :] * k_ref[...]) * decay
            attn = _dot(prod.astype(BF16), expand)
            oi_ref[pl.ds(t, 1), :] = jnp.sum(attn * v_ref[...], axis=0, keepdims=True)
            return carry2

        lax.fori_loop(0, c, one_row, 0, unroll=min(c, 8))

        st = st_ref[...]
        o = oi_ref[...] + _dot_nt((q * jnp.exp(cum)).astype(BF16), st.astype(BF16))
        last = cum[c - 1:c, :]
        kd = k * jnp.exp(last - cum)
        upd = _dot_tn(v.astype(BF16), kd.astype(BF16))
        st_ref[...] = st * jnp.exp(last) + jnp.where(state_mask, upd, 0.0)

        ms = _dot(o * o, group_mean, HI)
        y = o * lax.rsqrt(ms + NORM_EPS) * ng_ref[...] * (r * _sigmoid(r))
        o_ref[0, pl.ds(base, c), :] = y
        return carry

    lax.fori_loop(0, n_chunks, one_chunk, 0)

    @pl.when(pl.program_id(1) == pl.num_programs(1) - 1)
    def _():
        sout_ref[0] = st_ref[...]


def gla_mixer(z, wa, ba, ng, s0t, chunk):
    b, t, w = z.shape
    tb = _tile(t, SEQ_TILE)
    hk, hv = N_HEADS * GLA_DK, N_HEADS * GLA_DV
    kern = functools.partial(_gla_kernel, chunk=chunk, n_chunks=tb // chunk)
    return pl.pallas_call(
        kern,
        grid=(b, t // tb),
        in_specs=[pl.BlockSpec((1, tb, w), lambda i, j: (i, j, 0)),
                  pl.BlockSpec((LANES, hk), lambda i, j: (0, 0)),
                  pl.BlockSpec((1, hk), lambda i, j: (0, 0)),
                  pl.BlockSpec((1, hv), lambda i, j: (0, 0)),
                  pl.BlockSpec((1, hv, hk), lambda i, j: (i, 0, 0))],
        out_specs=[pl.BlockSpec((1, tb, hv), lambda i, j: (i, j, 0)),
                   pl.BlockSpec((1, hv, hk), lambda i, j: (i, 0, 0))],
        out_shape=[jax.ShapeDtypeStruct((b, t, hv), F32), jax.ShapeDtypeStruct((b, hv, hk), F32)],
        scratch_shapes=[pltpu.VMEM((hv, hk), F32), pltpu.VMEM((chunk, hk), F32), pltpu.VMEM((chunk, hk), F32),
                        pltpu.VMEM((chunk, hk), F32), pltpu.VMEM((chunk, hv), F32), pltpu.VMEM((chunk, hv), F32)],
        compiler_params=_cparams(("parallel", "arbitrary")),
        name="gla_mixer",
    )(z, wa, ba, ng, s0t)


def _unit_lower_inverse(m, c):
    eye = (_iota((c, c), 0) == _iota((c, c), 1)).astype(F32)
    p = -m
    inv = eye + p
    for _ in range(int(math.log2(c)) - 1):
        p = _dot(p, p, HI)
        inv = inv + _dot(inv, p, HI)
    return inv


def _gdn_kernel(z_ref, cw_ref, buf0_ref, alog_ref, dtb_ref, ng_ref, s0_ref, o_ref, sout_ref,
                s_ref, tail_ref, *, chunk, n_chunks):
    c = chunk
    w3 = 3 * GDN_W

    @pl.when(pl.program_id(1) == 0)
    def _():
        s_ref[...] = s0_ref[0]
        tail_ref[...] = buf0_ref[0]

    tril = (_iota((c, c), 1) <= _iota((c, c), 0)).astype(F32)
    triu = (_iota((c, c), 0) <= _iota((c, c), 1)).astype(F32)
    incl = _iota((c, c), 1) <= _iota((c, c), 0)
    strict = _iota((c, c), 1) < _iota((c, c), 0)
    group_ones = _group_ones(GDN_W, GDN_DH)
    row8 = _iota((SUBLANES, w3), 0)
    a_neg = -jnp.exp(alog_ref[...])

    def one_chunk(ci, carry):
        base = pl.multiple_of(ci * c, c)
        zc = z_ref[0, pl.ds(base, c), :]
        x = zc[:, 0:w3]
        gate = zc[:, w3:w3 + GDN_W]
        ab = zc[:, w3 + GDN_W:w3 + GDN_W + LANES]
        tail = tail_ref[...]
        conv = x * cw_ref[GDN_CONV - 1:GDN_CONV, :]
        for d in range(1, GDN_CONV):
            rolled = pltpu.roll(x, d, 0)
            head = jnp.where(row8 < d, pltpu.roll(tail, d, 0), rolled[0:SUBLANES, :])
            if c > SUBLANES:
                delayed = jnp.concatenate([head, rolled[SUBLANES:, :]], axis=0)
            else:
                delayed = head
            conv = conv + delayed * cw_ref[GDN_CONV - 1 - d:GDN_CONV - d, :]
        tail_ref[...] = x[c - SUBLANES:c, :]
        cv = conv * _sigmoid(conv)
        q = cv[:, 0:GDN_W]
        k = cv[:, GDN_W:2 * GDN_W]
        v = cv[:, 2 * GDN_W:3 * GDN_W]
        q = q * lax.rsqrt(_dot(q * q, group_ones, HI) + NORM_EPS) * (GDN_DH ** -0.5)
        k = k * lax.rsqrt(_dot(k * k, group_ones, HI) + NORM_EPS)
        log_alpha = a_neg * _softplus(ab + dtb_ref[...])
        log_alpha = jnp.where(_iota((c, LANES), 1) < N_HEADS, log_alpha, 0.0)
        beta = _sigmoid(ab)
        g = _dot(tril, log_alpha, HI)
        g_t = _dot_tn(log_alpha, triu, HI)
        eg = jnp.exp(g)
        outs = []
        for h in range(N_HEADS):
            sl = slice(h * GDN_DH, (h + 1) * GDN_DH)
            gh = g[:, h:h + 1]
            lmat = jnp.where(incl, jnp.exp(gh - g_t[h:h + 1, :]), 0.0)
            bh = beta[:, N_HEADS + h:N_HEADS + h + 1]
            qh, kh, vh = q[:, sl], k[:, sl], v[:, sl]
            kb = kh * bh
            m = jnp.where(strict, _dot_nt(kb, kh, HI) * lmat, 0.0)
            inv = _unit_lower_inverse(m, c)
            rhs = jnp.concatenate([vh * bh, kb * eg[:, h:h + 1]], axis=1)
            sol = _dot(inv, rhs, HI)
            u, w = sol[:, 0:GDN_DH], sol[:, GDN_DH:2 * GDN_DH]
            s = s_ref[h]
            v_new = u - _dot(w, s, HI)
            attn = _dot_nt(qh, kh, HI) * lmat
            outs.append(_dot(qh * eg[:, h:h + 1], s, HI) + _dot(attn, v_new, HI))
            g_last = g[c - 1:c, h:h + 1]
            s_ref[h] = s * jnp.exp(g_last) + _dot_tn(kh * jnp.exp(g_last - gh), v_new, HI)
        o = jnp.concatenate(outs, axis=1)
        ms = _dot(o * o, group_ones * (1.0 / GDN_DH), HI)
        o_ref[0, pl.ds(base, c), :] = o * lax.rsqrt(ms + NORM_EPS) * ng_ref[...] * (gate * _sigmoid(gate))
        return carry

    lax.fori_loop(0, n_chunks, one_chunk, 0)

    @pl.when(pl.program_id(1) == pl.num_programs(1) - 1)
    def _():
        sout_ref[0] = s_ref[...]


def gdn_mixer(z, cw, buf0, alog, dtb, ng, s0, chunk):
    b, t, w = z.shape
    tb = _tile(t, SEQ_TILE)
    kern = functools.partial(_gdn_kernel, chunk=chunk, n_chunks=tb // chunk)
    return pl.pallas_call(
        kern,
        grid=(b, t // tb),
        in_specs=[pl.BlockSpec((1, tb, w), lambda i, j: (i, j, 0)),
                  pl.BlockSpec((GDN_CONV, 3 * GDN_W), lambda i, j: (0, 0)),
                  pl.BlockSpec((1, SUBLANES, 3 * GDN_W), lambda i, j: (i, 0, 0)),
                  pl.BlockSpec((1, LANES), lambda i, j: (0, 0)),
                  pl.BlockSpec((1, LANES), lambda i, j: (0, 0)),
                  pl.BlockSpec((1, GDN_W), lambda i, j: (0, 0)),
                  pl.BlockSpec((1, N_HEADS, GDN_DH, GDN_DH), lambda i, j: (i, 0, 0, 0))],
        out_specs=[pl.BlockSpec((1, tb, GDN_W), lambda i, j: (i, j, 0)),
                   pl.BlockSpec((1, N_HEADS, GDN_DH, GDN_DH), lambda i, j: (i, 0, 0, 0))],
        out_shape=[jax.ShapeDtypeStruct((b, t, GDN_W), F32),
                   jax.ShapeDtypeStruct((b, N_HEADS, GDN_DH, GDN_DH), F32)],
        scratch_shapes=[pltpu.VMEM((N_HEADS, GDN_DH, GDN_DH), F32), pltpu.VMEM((SUBLANES, 3 * GDN_W), F32)],
        compiler_params=_cparams(("parallel", "arbitrary")),
        name="gdn_mixer",
    )(z, cw, buf0, alog, dtb, ng, s0)


def _mla_prep_kernel(z_ref, cos_ref, sin_ref, qg_ref, kvg_ref, wn_ref, wr_ref, wk_ref,
                     q_ref, kc_ref, c_ref, kr_ref):
    z = z_ref[0]
    cq = z[:, 0:MLA_Q_LORA]
    ckv = z[:, MLA_Q_LORA:MLA_Q_LORA + MLA_KV_LORA]
    kr = z[:, MLA_Q_LORA + MLA_KV_LORA:MLA_Q_LORA + MLA_KV_LORA + LANES]
    cos, sin = cos_ref[...], sin_ref[...]
    cqn = _rms(cq, qg_ref[...]).astype(BF16)
    q_nope = _dot(cqn, wn_ref[...])
    q_rope = _dot(cqn, wr_ref[...])
    q_rope = q_rope * cos + pltpu.roll(q_rope, LANES // 2, 1) * sin
    k_rope = kr * cos + pltpu.roll(kr, LANES // 2, 1) * sin
    c = _rms(ckv, kvg_ref[...])
    c_ref[0] = c
    kr_ref[0] = k_rope
    kc_ref[0] = jnp.concatenate([c, k_rope], axis=1).astype(kc_ref.dtype)
    lane_head = (_iota(q_rope.shape, 1) // (MLA_ROPE // 2)) % N_HEADS
    for h in range(N_HEADS):
        q_lat = _dot(q_nope[:, h * MLA_NOPE:(h + 1) * MLA_NOPE].astype(BF16), wk_ref[h])
        q_r = jnp.where(lane_head == h, q_rope, 0.0)
        q_ref[0, 0, h] = (jnp.concatenate([q_lat, q_r], axis=1) * MLA_SCALE).astype(q_ref.dtype)


def mla_prep(z, cos, sin, qg, kvg, wn, wr, wk, act_dtype):
    b, t, w = z.shape
    tm = _tile(t, ROW_TILE)
    dk = MLA_KV_LORA + LANES
    return pl.pallas_call(
        _mla_prep_kernel,
        grid=(b, t // tm),
        in_specs=[pl.BlockSpec((1, tm, w), lambda i, j: (i, j, 0)),
                  pl.BlockSpec((tm, LANES), lambda i, j: (j, 0)),
                  pl.BlockSpec((tm, LANES), lambda i, j: (j, 0)),
                  pl.BlockSpec((1, MLA_Q_LORA), lambda i, j: (0, 0)),
                  pl.BlockSpec((1, MLA_KV_LORA), lambda i, j: (0, 0)),
                  pl.BlockSpec(wn.shape, lambda i, j: (0, 0)),
                  pl.BlockSpec(wr.shape, lambda i, j: (0, 0)),
                  pl.BlockSpec(wk.shape, lambda i, j: (0, 0, 0))],
        out_specs=[pl.BlockSpec((1, 1, N_HEADS, tm, dk), lambda i, j: (i, 0, 0, j, 0)),
                   pl.BlockSpec((1, tm, dk), lambda i, j: (i, j, 0)),
                   pl.BlockSpec((1, tm, MLA_KV_LORA), lambda i, j: (i, j, 0)),
                   pl.BlockSpec((1, tm, LANES), lambda i, j: (i, j, 0))],
        out_shape=[jax.ShapeDtypeStruct((b, 1, N_HEADS, t, dk), act_dtype),
                   jax.ShapeDtypeStruct((b, t, dk), act_dtype),
                   jax.ShapeDtypeStruct((b, t, MLA_KV_LORA), F32),
                   jax.ShapeDtypeStruct((b, t, LANES), F32)],
        compiler_params=_cparams(("parallel", "parallel")),
        name="mla_prep",
    )(z, cos, sin, qg, kvg, wn, wr, wk)


def _split3(x):
    hi = x.astype(BF16).astype(F32)
    r1 = x - hi
    mid = r1.astype(BF16).astype(F32)
    lo = (r1 - mid).astype(BF16).astype(F32)
    return hi, mid, lo


def _fox_prep_kernel(z_ref, bf_ref, q_ref, k_ref, v_ref, lf_ref, carry_ref):
    tm = z_ref.shape[1]

    @pl.when(pl.program_id(1) == 0)
    def _():
        carry_ref[...] = jnp.zeros_like(carry_ref)

    z = z_ref[0]
    w = N_HEADS * FOX_DH
    f = z[:, 3 * w:3 * w + LANES]
    logf = jnp.where(_iota((tm, LANES), 1) < N_HEADS, _log_sigmoid(f + bf_ref[...]), 0.0)
    lf_ref[0] = logf
    tril = (_iota((tm, tm), 1) <= _iota((tm, tm), 0)).astype(F32)
    cum = _dot(tril, logf, HI) + carry_ref[...]
    carry_ref[...] = cum[tm - 1:tm, :]
    hi, mid, lo = _split3(cum)
    lane = _iota((tm, FOX_DH), 1)
    ones = jnp.where(lane < 3, 1.0, 0.0)
    for h in range(N_HEADS):
        ch = [p[:, h:h + 1] for p in (hi, mid, lo)]
        c_q = jnp.where(lane == 0, ch[0], jnp.where(lane == 1, ch[1], jnp.where(lane == 2, ch[2], 0.0)))
        c_k = jnp.where(lane == 3, -ch[0], jnp.where(lane == 4, -ch[1], jnp.where(lane == 5, -ch[2], 0.0)))
        q_tail = c_q + jnp.where((lane >= 3) & (lane < 6), 1.0, 0.0)
        k_tail = c_k + ones
        qh = z[:, h * FOX_DH:(h + 1) * FOX_DH] * FOX_SCALE
        kh = z[:, w + h * FOX_DH:w + (h + 1) * FOX_DH]
        q_ref[0, h, 0] = jnp.concatenate([qh, q_tail], axis=1).astype(BF16)
        k_ref[0, h] = jnp.concatenate([kh, k_tail], axis=1).astype(BF16)
        v_ref[0, h] = z[:, 2 * w + h * FOX_DH:2 * w + (h + 1) * FOX_DH].astype(BF16)


def fox_prep(z, bf):
    b, t, w = z.shape
    tm = _tile(t, 256)
    dk = 2 * FOX_DH
    return pl.pallas_call(
        _fox_prep_kernel,
        grid=(b, t // tm),
        in_specs=[pl.BlockSpec((1, tm, w), lambda i, j: (i, j, 0)),
                  pl.BlockSpec((1, LANES), lambda i, j: (0, 0))],
        out_specs=[pl.BlockSpec((1, N_HEADS, 1, tm, dk), lambda i, j: (i, 0, 0, j, 0)),
                   pl.BlockSpec((1, N_HEADS, tm, dk), lambda i, j: (i, 0, j, 0)),
                   pl.BlockSpec((1, N_HEADS, tm, FOX_DH), lambda i, j: (i, 0, j, 0)),
                   pl.BlockSpec((1, tm, LANES), lambda i, j: (i, j, 0))],
        out_shape=[jax.ShapeDtypeStruct((b, N_HEADS, 1, t, dk), BF16),
                   jax.ShapeDtypeStruct((b, N_HEADS, t, dk), BF16),
                   jax.ShapeDtypeStruct((b, N_HEADS, t, FOX_DH), BF16),
                   jax.ShapeDtypeStruct((b, t, LANES), F32)],
        scratch_shapes=[pltpu.VMEM((1, LANES), F32)],
        compiler_params=_cparams(("parallel", "arbitrary")),
        name="fox_prep",
    )(z, bf)


def _flash_kernel(q_ref, k_ref, v_ref, o_ref, m_ref, l_ref, acc_ref, *, hs, tq):
    qi, ki = pl.program_id(2), pl.program_id(3)
    rows = hs * tq

    @pl.when(ki == 0)
    def _():
        m_ref[...] = jnp.full_like(m_ref, -jnp.inf)
        l_ref[...] = jnp.zeros_like(l_ref)
        acc_ref[...] = jnp.zeros_like(acc_ref)

    def update(masked):
        q = q_ref[0, 0].reshape(rows, q_ref.shape[-1])
        s = _dot_nt(q, k_ref[0, 0])
        if masked:
            qpos = _iota(s.shape, 0) % tq
            s = jnp.where(qpos >= _iota(s.shape, 1), s, -jnp.inf)
        m_old = m_ref[...]
        m_new = jnp.maximum(m_old, jnp.max(s, axis=-1, keepdims=True))
        alpha = jnp.exp(m_old - m_new)
        p = jnp.exp(s - m_new)
        l_ref[...] = alpha * l_ref[...] + jnp.sum(p, axis=-1, keepdims=True)
        acc_ref[...] = alpha * acc_ref[...] + _dot(p.astype(v_ref.dtype), v_ref[0, 0])
        m_ref[...] = m_new

    @pl.when(ki < qi)
    def _():
        update(False)

    @pl.when(ki == qi)
    def _():
        update(True)
        o = acc_ref[...] / l_ref[...]
        o_ref[0, 0] = o.reshape(hs, tq, o.shape[-1])


def flash_causal(q, k, v, dv):
    b, g, hs, t, dk = q.shape
    tq = _tile(t, ATTN_TILE)
    nq = t // tq
    kern = functools.partial(_flash_kernel, hs=hs, tq=tq)
    return pl.pallas_call(
        kern,
        grid=(b, g, nq, nq),
        in_specs=[pl.BlockSpec((1, 1, hs, tq, dk), lambda bi, gi, qi, ki: (bi, gi, 0, qi, 0)),
                  pl.BlockSpec((1, 1, tq, dk), lambda bi, gi, qi, ki: (bi, gi, jnp.minimum(ki, qi), 0)),
                  pl.BlockSpec((1, 1, tq, dv), lambda bi, gi, qi, ki: (bi, gi, jnp.minimum(ki, qi), 0))],
        out_specs=pl.BlockSpec((1, 1, hs, tq, dv), lambda bi, gi, qi, ki: (bi, gi, 0, qi, 0)),
        out_shape=jax.ShapeDtypeStruct((b, g, hs, t, dv), F32),
        scratch_shapes=[pltpu.VMEM((hs * tq, 1), F32), pltpu.VMEM((hs * tq, 1), F32),
                        pltpu.VMEM((hs * tq, dv), F32)],
        compiler_params=_cparams(("parallel", "parallel", "parallel", "arbitrary")),
        name="flash_causal",
    )(q, k, v)


def _online_update(scores, values, m_ref, l_ref, acc_ref):
    s = jnp.concatenate(scores, axis=1)
    m_old = m_ref[...]
    m_new = jnp.maximum(m_old, jnp.max(s, axis=-1, keepdims=True))
    alpha = jnp.exp(m_old - m_new)
    p = jnp.exp(s - m_new)
    l_ref[...] = alpha * l_ref[...] + jnp.sum(p, axis=-1, keepdims=True)
    pv = None
    for i, v in enumerate(values):
        term = _dot(p[:, i * PAGE:(i + 1) * PAGE].astype(BF16), v)
        pv = term if pv is None else pv + term
    acc_ref[...] = alpha * acc_ref[...] + pv
    m_ref[...] = m_new


def _mla_decode_kernel(pt_ref, q_ref, kn_ref, *refs, n_new, pages):
    lat_refs, rope_refs = refs[:pages], refs[pages:2 * pages]
    o_ref, m_ref, l_ref, acc_ref = refs[2 * pages:]
    j = pl.program_id(1)
    rows = N_HEADS * n_new
    q = q_ref[0, 0].reshape(rows, q_ref.shape[-1])
    q_lat = q[:, 0:MLA_KV_LORA].astype(BF16)
    half = MLA_ROPE // 2
    sr, sc = _iota((LANES, MLA_ROPE), 0), _iota((LANES, MLA_ROPE), 1)
    compact = ((sr % half == sc % half) & ((sr >= LANES // 2) == (sc >= half))).astype(BF16)
    q_rope = _dot(q[:, MLA_KV_LORA:].astype(BF16), compact).astype(BF16)

    @pl.when(j == 0)
    def _():
        kn = kn_ref[0].astype(BF16)
        s = _dot_nt(q.astype(BF16), kn)
        s = jnp.where(_iota(s.shape, 0) % n_new >= _iota(s.shape, 1), s, -jnp.inf)
        m = jnp.max(s, axis=-1, keepdims=True)
        p = jnp.exp(s - m)
        m_ref[...] = m
        l_ref[...] = jnp.sum(p, axis=-1, keepdims=True)
        acc_ref[...] = _dot(p.astype(BF16), kn[:, 0:MLA_KV_LORA])

    scores, values = [], []
    for i in range(pages):
        c = lat_refs[i][0].astype(BF16)
        r = rope_refs[i][0].astype(BF16)
        scores.append(_dot_nt(q_lat, c) + _dot_nt(q_rope, r))
        values.append(c)
    _online_update(scores, values, m_ref, l_ref, acc_ref)

    @pl.when(j == pl.num_programs(1) - 1)
    def _():
        o = acc_ref[...] / l_ref[...]
        o_ref[0, 0] = o.reshape(N_HEADS, n_new, MLA_KV_LORA)


def mla_decode(q, kc_new, lat_pool, rope_pool, page_table, page0):
    b, _, _, n_new, dk = q.shape
    n_pages = page_table.shape[1]
    pages = _tile(n_pages, DEC_PAGES)

    def page_map(i):
        return lambda bi, j, pt: (page0 + pt[bi, j * pages + i], 0, 0)

    in_specs = [pl.BlockSpec((1, 1, N_HEADS, n_new, dk), lambda bi, j, pt: (bi, 0, 0, 0, 0)),
                pl.BlockSpec((1, n_new, dk), lambda bi, j, pt: (bi, 0, 0))]
    in_specs += [pl.BlockSpec((1, PAGE, MLA_KV_LORA), page_map(i)) for i in range(pages)]
    in_specs += [pl.BlockSpec((1, PAGE, MLA_ROPE), page_map(i)) for i in range(pages)]
    rows = N_HEADS * n_new
    kern = functools.partial(_mla_decode_kernel, n_new=n_new, pages=pages)
    return pl.pallas_call(
        kern,
        grid_spec=pltpu.PrefetchScalarGridSpec(
            num_scalar_prefetch=1,
            grid=(b, n_pages // pages),
            in_specs=in_specs,
            out_specs=pl.BlockSpec((1, 1, N_HEADS, n_new, MLA_KV_LORA), lambda bi, j, pt: (bi, 0, 0, 0, 0)),
            scratch_shapes=[pltpu.VMEM((rows, 1), F32), pltpu.VMEM((rows, 1), F32),
                            pltpu.VMEM((rows, MLA_KV_LORA), F32)]),
        out_shape=jax.ShapeDtypeStruct((b, 1, N_HEADS, n_new, MLA_KV_LORA), F32),
        compiler_params=_cparams(("parallel", "arbitrary")),
        name="mla_decode",
    )(page_table, q, kc_new, *([lat_pool] * pages), *([rope_pool] * pages))


def _fox_decode_kernel(pt_ref, z_ref, bf_ref, *refs, n_new, pages):
    k_refs, v_refs, f_refs = refs[:pages], refs[pages:2 * pages], refs[2 * pages:3 * pages]
    o_ref, lf_ref, m_ref, l_ref, acc_ref, carry_ref = refs[3 * pages:]
    j = pl.program_id(1)
    w = N_HEADS * FOX_DH
    rows = N_HEADS * n_new
    z = z_ref[0]
    q = z[:, 0:w] * FOX_SCALE
    f = z[:, 3 * w:3 * w + LANES]
    logf = jnp.where(_iota((n_new, LANES), 1) < N_HEADS, _log_sigmoid(f + bf_ref[...]), 0.0)
    tril = (_iota((n_new, n_new), 1) <= _iota((n_new, n_new), 0)).astype(F32)
    triu = (_iota((n_new, n_new), 0) <= _iota((n_new, n_new), 1)).astype(F32)
    cn = _dot(tril, logf, HI)
    cn_t = _dot_tn(logf, triu, HI)
    block = _iota((rows, w), 0) // n_new == _iota((rows, w), 1) // FOX_DH
    q_bd = jnp.where(block, jnp.concatenate([q] * N_HEADS, axis=0), 0.0).astype(BF16)
    cn_col = jnp.concatenate([cn[:, h:h + 1] for h in range(N_HEADS)], axis=0)

    @pl.when(j == 0)
    def _():
        lf_ref[0] = logf
        kn = z[:, w:2 * w].astype(BF16)
        vn = z[:, 2 * w:3 * w].astype(BF16)
        cn_row = jnp.concatenate([jnp.broadcast_to(cn_t[h:h + 1, :], (n_new, n_new)) for h in range(N_HEADS)], axis=0)
        s = _dot_nt(q_bd, kn) + cn_col - cn_row
        s = jnp.where(_iota(s.shape, 0) % n_new >= _iota(s.shape, 1), s, -jnp.inf)
        m = jnp.max(s, axis=-1, keepdims=True)
        p = jnp.exp(s - m)
        m_ref[...] = m
        l_ref[...] = jnp.sum(p, axis=-1, keepdims=True)
        acc_ref[...] = _dot(p.astype(BF16), vn)
        carry_ref[...] = jnp.zeros_like(carry_ref)

    lp = jnp.concatenate([f_refs[i][0] for i in range(pages)], axis=0)
    later = (_iota((PAGE, PAGE), 0) > _iota((PAGE, PAGE), 1)).astype(F32)
    after = _dot(lp, later, HI)
    total = jnp.sum(lp, axis=1, keepdims=True)
    carry = carry_ref[...]
    scores, values = [], []
    for i in range(pages):
        after_i = after[i * N_HEADS:(i + 1) * N_HEADS, :] + carry
        carry = carry + total[i * N_HEADS:(i + 1) * N_HEADS, :]
        bias = jnp.concatenate([jnp.broadcast_to(after_i[h:h + 1, :], (n_new, PAGE)) for h in range(N_HEADS)],
                               axis=0)
        scores.append(_dot_nt(q_bd, k_refs[i][0].astype(BF16)) + cn_col + bias)
        values.append(v_refs[i][0].astype(BF16))
    carry_ref[...] = carry
    _online_update(scores, values, m_ref, l_ref, acc_ref)

    @pl.when(j == pl.num_programs(1) - 1)
    def _():
        o = jnp.where(block, acc_ref[...] / l_ref[...], 0.0)
        out = o[0:n_new, :]
        for h in range(1, N_HEADS):
            out = out + o[h * n_new:(h + 1) * n_new, :]
        o_ref[0] = out


def fox_decode(z, bf, k_pool, v_pool, f_pool, page_table, page0):
    b, n_new, zw = z.shape
    n_pages = page_table.shape[1]
    pages = _tile(n_pages, DEC_PAGES)
    w = N_HEADS * FOX_DH

    def page_map(i):
        return lambda bi, j, pt: (page0 + pt[bi, n_pages - 1 - (j * pages + i)], 0, 0)

    in_specs = [pl.BlockSpec((1, n_new, zw), lambda bi, j, pt: (bi, 0, 0)),
                pl.BlockSpec((1, LANES), lambda bi, j, pt: (0, 0))]
    in_specs += [pl.BlockSpec((1, PAGE, w), page_map(i)) for i in range(pages)]
    in_specs += [pl.BlockSpec((1, PAGE, w), page_map(i)) for i in range(pages)]
    in_specs += [pl.BlockSpec((1, N_HEADS, PAGE), page_map(i)) for i in range(pages)]
    rows = N_HEADS * n_new
    kern = functools.partial(_fox_decode_kernel, n_new=n_new, pages=pages)
    return pl.pallas_call(
        kern,
        grid_spec=pltpu.PrefetchScalarGridSpec(
            num_scalar_prefetch=1,
            grid=(b, n_pages // pages),
            in_specs=in_specs,
            out_specs=[pl.BlockSpec((1, n_new, w), lambda bi, j, pt: (bi, 0, 0)),
                       pl.BlockSpec((1, n_new, LANES), lambda bi, j, pt: (bi, 0, 0))],
            scratch_shapes=[pltpu.VMEM((rows, 1), F32), pltpu.VMEM((rows, 1), F32),
                            pltpu.VMEM((rows, w), F32), pltpu.VMEM((N_HEADS, 1), F32)]),
        out_shape=[jax.ShapeDtypeStruct((b, n_new, w), F32), jax.ShapeDtypeStruct((b, n_new, LANES), F32)],
        compiler_params=_cparams(("parallel", "arbitrary")),
        name="fox_decode",
    )(page_table, z, bf, *([k_pool] * pages), *([v_pool] * pages), *([f_pool] * pages))


def _merge_kernel(x_ref, oa_ref, olat_ref, oc_ref, od_ref, gate_ref, wuv_ref, wb_ref, wo_ref, h_ref):
    d = x_ref.shape[1]
    olat = olat_ref[...]
    ob = jnp.concatenate([_dot(olat[:, h * MLA_KV_LORA:(h + 1) * MLA_KV_LORA].astype(BF16), wuv_ref[h])
                          for h in range(N_HEADS)], axis=1)
    acc = None
    for n, o in enumerate((oa_ref[...], ob, oc_ref[...], od_ref[...])):
        term = _sigmoid(gate_ref[:, n * d:(n + 1) * d]) * _dot(o.astype(BF16), wb_ref[n])
        acc = term if acc is None else acc + term
    h_ref[...] = x_ref[...] + _dot(acc.astype(BF16), wo_ref[...])


def merge(x, oa, olat, oc, od, gate, wuv, wb, wo):
    n, d = x.shape
    tm = _tile(n, ROW_TILE)
    row = lambda width: pl.BlockSpec((tm, width), lambda i: (i, 0))
    return pl.pallas_call(
        _merge_kernel,
        grid=(n // tm,),
        in_specs=[row(d), row(BRANCH_W), row(N_HEADS * MLA_KV_LORA), row(BRANCH_W), row(BRANCH_W), row(4 * d),
                  pl.BlockSpec(wuv.shape, lambda i: (0, 0, 0)),
                  pl.BlockSpec(wb.shape, lambda i: (0, 0, 0)),
                  pl.BlockSpec(wo.shape, lambda i: (0, 0))],
        out_specs=row(d),
        out_shape=jax.ShapeDtypeStruct((n, d), F32),
        compiler_params=_cparams(("parallel",)),
        name="merge",
    )(x, oa, olat, oc, od, gate, wuv, wb, wo)


def _peer_select_kernel(h_ref, g_ref, wq_ref, sk_ref, pa_ref, pb_ref, xn_ref, idx_ref, gate_ref, q_ref, sc_ref):
    hd = pl.program_id(1)
    tm = h_ref.shape[0]
    qd = 2 * PEER_HALF

    @pl.when(hd == 0)
    def _():
        xn = _rms(h_ref[...], g_ref[...])
        xn_ref[...] = xn
        q = _dot(xn, wq_ref[...], HI)
        for i in range(PEER_HEADS):
            q_ref[i] = q[:, i * qd:(i + 1) * qd]
        idx_ref[...] = jnp.zeros_like(idx_ref)
        sc_ref[...] = jnp.zeros_like(sc_ref)

    qh = q_ref[hd]
    lane = _iota((tm, PEER_NKEYS), 1)
    sub0 = _dot_nt(qh[:, 0:PEER_HALF], sk_ref[0, 0], HI)
    sub1 = _dot_nt(qh[:, PEER_HALF:2 * PEER_HALF], sk_ref[0, 1], HI)

    lane_f = lane.astype(F32)

    def first_max(w):
        m = jnp.max(w, axis=-1, keepdims=True)
        return jnp.min(jnp.where(w == m, lane_f, float(PEER_NKEYS)), axis=-1, keepdims=True)

    def pick(a, carry):
        w0, w1, pos = carry
        a0, a1 = first_max(w0), first_max(w1)
        pos = jnp.where(lane == a, a0, jnp.where(lane == PEER_TOPK + a, a1, pos))
        return jnp.where(lane_f == a0, -jnp.inf, w0), jnp.where(lane_f == a1, -jnp.inf, w1), pos

    _, _, pos = lax.fori_loop(0, PEER_TOPK, pick, (sub0, sub1, jnp.zeros((tm, PEER_NKEYS), F32)))
    pos = pos.astype(I32)
    vals = jnp.where(lane < PEER_TOPK, jnp.take_along_axis(sub0, pos, axis=1),
                     jnp.where(lane < 2 * PEER_TOPK, jnp.take_along_axis(sub1, pos, axis=1), 0.0))

    pa, pb = pa_ref[...], pb_ref[...]
    er = _iota((PEER_NKEYS, PEER_NKEYS), 0)
    spread0 = (er == pa).astype(BF16)
    spread1 = (er - PEER_TOPK == pb).astype(BF16)

    def spread(x, e):
        hi, mid, lo = _split3(x)
        return (_dot(hi.astype(BF16), e) + _dot(mid.astype(BF16), e)) + _dot(lo.astype(BF16), e)

    cand_s = jnp.where(pa >= 0, spread(vals, spread0) + spread(vals, spread1), -jnp.inf)
    idb = jnp.where(lane < 2 * PEER_TOPK, pos, 0).astype(F32).astype(BF16)
    cand_i = _dot(idb, spread0) * float(PEER_NKEYS) + _dot(idb, spread1)

    def best(kk, carry):
        w, where_ = carry
        am = first_max(w)
        return jnp.where(lane_f == am, -jnp.inf, w), jnp.where(lane == hd * PEER_TOPK + kk, am, where_)

    _, where_ = lax.fori_loop(0, PEER_TOPK, best, (cand_s, jnp.zeros((tm, PEER_NKEYS), F32)))
    where_ = where_.astype(I32)
    mine = lane // PEER_TOPK == hd
    sc = jnp.where(mine, jnp.take_along_axis(cand_s, where_, axis=1), sc_ref[...])
    ix = jnp.where(mine, jnp.take_along_axis(cand_i, where_, axis=1).astype(I32), idx_ref[...])
    sc_ref[...] = sc
    idx_ref[...] = ix

    @pl.when(hd == pl.num_programs(1) - 1)
    def _():
        grp = lane // PEER_TOPK
        mx = jnp.zeros((tm, PEER_NKEYS), F32)
        for i in range(PEER_HEADS):
            mx = jnp.where(grp == i, jnp.max(jnp.where(grp == i, sc, -jnp.inf), axis=-1, keepdims=True), mx)
        e = jnp.exp(sc - mx)
        den = _dot(e, _group_ones(PEER_NKEYS, PEER_TOPK), HI)
        gate_ref[...] = e / den


def peer_select(h, g, wq, sk):
    n, d = h.shape
    tm = _tile(n, SELECT_TILE)
    nk = PEER_HEADS * PEER_TOPK
    pairs = [(a, b) for a in range(PEER_TOPK) for b in range(PEER_TOPK) if (a + 1) * (b + 1) <= PEER_TOPK]
    assert len(pairs) <= PEER_NKEYS
    pad = [-1] * (PEER_NKEYS - len(pairs))
    pa = jnp.asarray([[a for a, _ in pairs] + pad], I32)
    pb = jnp.asarray([[b for _, b in pairs] + pad], I32)
    return pl.pallas_call(
        _peer_select_kernel,
        grid=(n // tm, PEER_HEADS),
        in_specs=[pl.BlockSpec((tm, d), lambda i, j: (i, 0)),
                  pl.BlockSpec((1, d), lambda i, j: (0, 0)),
                  pl.BlockSpec(wq.shape, lambda i, j: (0, 0)),
                  pl.BlockSpec((1, 2, PEER_NKEYS, PEER_HALF), lambda i, j: (j, 0, 0, 0)),
                  pl.BlockSpec((1, PEER_NKEYS), lambda i, j: (0, 0)),
                  pl.BlockSpec((1, PEER_NKEYS), lambda i, j: (0, 0))],
        out_specs=[pl.BlockSpec((tm, d), lambda i, j: (i, 0)),
                   pl.BlockSpec((tm, nk), lambda i, j: (i, 0)),
                   pl.BlockSpec((tm, nk), lambda i, j: (i, 0))],
        out_shape=[jax.ShapeDtypeStruct((n, d), F32), jax.ShapeDtypeStruct((n, nk), I32),
                   jax.ShapeDtypeStruct((n, nk), F32)],
        scratch_shapes=[pltpu.VMEM((PEER_HEADS, tm, 2 * PEER_HALF), F32), pltpu.VMEM((tm, nk), F32)],
        compiler_params=_cparams(("parallel", "arbitrary")),
        name="peer_select",
    )(h, g, wq, sk, pa, pb)


def _gelu(x):
    return 0.5 * x * (1.0 + lax.erf(x * (2.0 ** -0.5)))


def _peer_gather_kernel(idx_ref, idxn_ref, gate_ref, xn_ref, h_ref, uv_hbm, y_ref, buf, sem, wb_ref, *, tokens):
    i = pl.program_id(0)
    n = pl.num_programs(0)
    nk = PEER_HEADS * PEER_TOPK
    picks = tokens * nk

    rpe = 2 * SUBLANES

    def issue(ids_ref, slot):
        for r in range(picks):
            copy = pltpu.make_async_copy(uv_hbm.at[ids_ref[r]], buf.at[slot, pl.ds(r * rpe, rpe), :], sem.at[slot])
            copy.start(priority=r % 2)

    def compute(slot):
        pltpu.make_async_copy(buf.at[slot], buf.at[slot], sem.at[slot]).wait()
        gate_t = jnp.transpose(gate_ref[...])
        for t in range(tokens):
            base = t * nk * rpe
            u = jnp.concatenate([buf[slot, pl.ds(base + s, nk, stride=rpe), :] for s in range(SUBLANES)],
                                axis=1)
            x8 = jnp.broadcast_to(xn_ref[t:t + 1, :], (SUBLANES, u.shape[1]))
            act = _dot_nt(u, x8)[:, 0:1]
            wgt = gate_t[:, t:t + 1] * _gelu(act)
            wb_ref[...] = jnp.broadcast_to(wgt, (nk, LANES))
            parts = [None] * 4
            for jj in range(nk):
                term = wb_ref[jj:jj + 1, :] * buf[slot, pl.ds(base + jj * rpe + SUBLANES, SUBLANES), :]
                parts[jj % 4] = term if parts[jj % 4] is None else parts[jj % 4] + term
            y_ref[t] = h_ref[t] + ((parts[0] + parts[1]) + (parts[2] + parts[3]))

    @pl.when(i == 0)
    def _():
        issue(idx_ref, 0)

    for s in range(2):
        @pl.when(jnp.logical_and(i + 1 < n, (i + 1) % 2 == s))
        def _(s=s):
            issue(idxn_ref, s)

    for s in range(2):
        @pl.when(i % 2 == s)
        def _(s=s):
            compute(s)


def peer_gather(idx, gate, xn, h, uv):
    n, d = h.shape
    tb = _tile(n, GATHER_TOKENS)
    nk = PEER_HEADS * PEER_TOPK
    steps = n // tb
    h3 = h.reshape(n, SUBLANES, d // SUBLANES)
    tile3 = pl.BlockSpec((tb, SUBLANES, d // SUBLANES), lambda i: (i, 0, 0))
    kern = functools.partial(_peer_gather_kernel, tokens=tb)
    y = pl.pallas_call(
        kern,
        grid=(steps,),
        in_specs=[pl.BlockSpec((tb * nk,), lambda i: (i,), memory_space=pltpu.SMEM),
                  pl.BlockSpec((tb * nk,), lambda i: (jnp.minimum(i + 1, steps - 1),), memory_space=pltpu.SMEM),
                  pl.BlockSpec((tb, nk), lambda i: (i, 0)),
                  pl.BlockSpec((tb, d), lambda i: (i, 0)),
                  tile3,
                  pl.BlockSpec(memory_space=pl.ANY)],
        out_specs=tile3,
        out_shape=jax.ShapeDtypeStruct(h3.shape, F32),
        scratch_shapes=[pltpu.VMEM((2, tb * nk * 2 * SUBLANES, d // SUBLANES), F32),
                        pltpu.SemaphoreType.DMA((2,)),
                        pltpu.VMEM((nk, nk), F32)],
        compiler_params=_cparams(("arbitrary",)),
        name="peer_gather",
    )(idx.reshape(n * nk), idx.reshape(n * nk), gate, xn, h3, uv)
    return y.reshape(n, d)


def _pad_cols(w, width):
    return jnp.pad(w, ((0, 0), (0, width - w.shape[1])))


def _layer_weights(l, p):
    d = p['w_in'].shape[1]
    w_in = p['w_in'][l]
    hk, hv = N_HEADS * GLA_DK, N_HEADS * GLA_DV
    off = 0
    cols = {}
    for name, width in (('gla_q', hk), ('gla_k', hk), ('gla_v', hv), ('gla_r', hv), ('gla_a', GLA_LR),
                        ('mla_cq', MLA_Q_LORA), ('mla_ckv', MLA_KV_LORA), ('mla_kr', MLA_ROPE),
                        ('fox_q', GDN_W), ('fox_k', GDN_W), ('fox_v', GDN_W), ('fox_f', N_HEADS),
                        ('gdn_qkv', 3 * GDN_W), ('gdn_a', N_HEADS), ('gdn_b', N_HEADS), ('gdn_g', GDN_W),
                        ('gate', 4 * d)):
        cols[name] = w_in[:, off:off + width]
        off += width
    assert off == w_in.shape[1]
    half = MLA_ROPE // 2
    kr = cols['mla_kr']
    kr_tiled = jnp.concatenate([jnp.tile(kr[:, :half], (1, N_HEADS)), jnp.tile(kr[:, half:], (1, N_HEADS))], axis=1)
    lw = {
        'w_gla': jnp.concatenate([cols['gla_q'], cols['gla_k'], cols['gla_v'], cols['gla_r'],
                                  _pad_cols(cols['gla_a'], LANES)], axis=1).astype(BF16),
        'w_mla': jnp.concatenate([cols['mla_cq'], cols['mla_ckv'], kr_tiled], axis=1).astype(BF16),
        'w_fox': jnp.concatenate([cols['fox_q'], cols['fox_k'], cols['fox_v'],
                                  _pad_cols(cols['fox_f'], LANES)], axis=1).astype(BF16),
        'w_gdn': jnp.concatenate([cols['gdn_qkv'], cols['gdn_g'],
                                  _pad_cols(jnp.concatenate([cols['gdn_a'], cols['gdn_b']], axis=1), LANES)],
                                 axis=1).astype(BF16),
        'w_gate': cols['gate'].astype(BF16),
    }
    lw['gla_wa'] = jnp.pad(p['gla_w_a2'][l], ((0, LANES - GLA_LR), (0, 0)))
    lw['gla_ba'] = p['gla_b_a'][l][None, :]
    lw['gla_ng'] = jnp.tile(p['gla_norm_g'][l], N_HEADS)[None, :]
    w_uq = p['mla_w_uq'][l]
    lw['mla_wn'] = w_uq[:, :, :MLA_NOPE].reshape(MLA_Q_LORA, N_HEADS * MLA_NOPE).astype(BF16)
    lw['mla_wr'] = jnp.concatenate([w_uq[:, :, MLA_NOPE:MLA_NOPE + half].reshape(MLA_Q_LORA, N_HEADS * half),
                                    w_uq[:, :, MLA_NOPE + half:].reshape(MLA_Q_LORA, N_HEADS * half)],
                                   axis=1).astype(BF16)
    lw['mla_wk'] = jnp.transpose(p['mla_w_uk'][l], (1, 2, 0)).astype(BF16)
    lw['mla_wuv'] = jnp.transpose(p['mla_w_uv'][l], (1, 0, 2)).astype(BF16)
    lw['mla_qg'] = p['mla_qnorm_g'][l][None, :]
    lw['mla_kvg'] = p['mla_kvnorm_g'][l][None, :]
    lw['fox_bf'] = _pad_cols(p['fox_b_f'][l][None, :], LANES)
    lw['gdn_cw'] = p['gdn_conv_w'][l]
    lw['gdn_alog'] = _pad_cols(p['gdn_a_log'][l][None, :], LANES)
    lw['gdn_dtb'] = _pad_cols(p['gdn_dt_bias'][l][None, :], LANES)
    lw['gdn_ng'] = jnp.tile(p['gdn_norm_g'][l], N_HEADS)[None, :]
    lw['wb'] = p['w_branch'][l].astype(BF16)
    lw['wo'] = p['w_out'][l].astype(BF16)
    lw['norm1_g'] = p['norm1_g'][l][None, :]
    lw['norm2_g'] = p['norm2_g'][l][None, :]
    lw['peer_wq'] = p['peer_w_q'][l]
    lw['peer_sk'] = p['peer_subkeys'][l]
    n_exp = p['peer_u'].shape[1]
    lw['peer_uv'] = jnp.concatenate([p['peer_u'][l].reshape(n_exp, SUBLANES, d // SUBLANES),
                                     p['peer_v'][l].reshape(n_exp, SUBLANES, d // SUBLANES)], axis=1)
    return lw


def _rope_tables(pos):
    half = MLA_ROPE // 2
    inv = ROPE_BASE ** (-jnp.arange(half, dtype=F32) / half)
    ang = pos.astype(F32)[:, None] * inv[None, :]
    cos, sin = jnp.cos(ang), jnp.sin(ang)
    cos_t = jnp.tile(cos, (1, 2 * N_HEADS))
    sin_t = jnp.concatenate([jnp.tile(-sin, (1, N_HEADS)), jnp.tile(sin, (1, N_HEADS))], axis=1)
    return cos_t, sin_t


def _gla_state_to_kernel(s):
    b = s.shape[0]
    eye = jnp.eye(N_HEADS, dtype=s.dtype)
    return jnp.einsum('bhdv,hg->bhvgd', s, eye).reshape(b, N_HEADS * GLA_DV, N_HEADS * GLA_DK)


def _gla_state_from_kernel(st):
    b = st.shape[0]
    s5 = st.reshape(b, N_HEADS, GLA_DV, N_HEADS, GLA_DK)
    diag = jnp.stack([s5[:, h, :, h, :] for h in range(N_HEADS)], axis=1)
    return jnp.swapaxes(diag, 2, 3)


def _token_mix(x, lw, pos, past, l):
    b, t, d = x.shape
    n = b * t
    x2 = x.reshape(n, d)
    g1 = lw['norm1_g']
    z_gla = norm_proj(x2, g1, lw['w_gla']).reshape(b, t, -1)
    z_mla = norm_proj(x2, g1, lw['w_mla']).reshape(b, t, -1)
    z_fox = norm_proj(x2, g1, lw['w_fox']).reshape(b, t, -1)
    z_gdn = norm_proj(x2, g1, lw['w_gdn']).reshape(b, t, -1)
    z_gate = norm_proj(x2, g1, lw['w_gate'])

    if past is None:
        gla_s0 = jnp.zeros((b, N_HEADS * GLA_DV, N_HEADS * GLA_DK), F32)
    else:
        gla_s0 = _gla_state_to_kernel(past['state_gla'][l])
    o_a, gla_st = gla_mixer(z_gla, lw['gla_wa'], lw['gla_ba'], lw['gla_ng'], gla_s0, min(GLA_CHUNK, t))
    gla_s = _gla_state_from_kernel(gla_st)

    assert t >= GDN_CONV - 1
    if past is None:
        gdn_s0 = jnp.zeros((b, N_HEADS, GDN_DH, GDN_DH), F32)
        buf0 = jnp.zeros((b, SUBLANES, 3 * GDN_W), F32)
    else:
        gdn_s0 = past['state_gdn'][l]
        buf0 = jnp.pad(past['state_gdn_conv'][l], ((0, 0), (SUBLANES - (GDN_CONV - 1), 0), (0, 0)))
    o_d, gdn_s = gdn_mixer(z_gdn, lw['gdn_cw'], buf0, lw['gdn_alog'], lw['gdn_dtb'], lw['gdn_ng'], gdn_s0,
                           min(GDN_CHUNK, t))
    gdn_buf = z_gdn[:, t - (GDN_CONV - 1):, :3 * GDN_W]

    cos_t, sin_t = _rope_tables(pos)
    act_dtype = BF16 if past is None else F32
    q_m, kc, c_lat, kr_t = mla_prep(z_mla, cos_t, sin_t, lw['mla_qg'], lw['mla_kvg'], lw['mla_wn'], lw['mla_wr'],
                                    lw['mla_wk'], act_dtype)
    half = MLA_ROPE // 2
    k_rope = jnp.concatenate([kr_t[..., :half], kr_t[..., LANES // 2:LANES // 2 + half]], axis=-1)
    if past is None:
        o_lat = flash_causal(q_m, kc[:, None], kc[:, None], MLA_KV_LORA)
    else:
        o_lat = mla_decode(q_m, kc, past['lat_pool'], past['rope_pool'], past['page_table'], l * past['n_pool'])
    o_lat = jnp.transpose(o_lat[:, 0], (0, 2, 1, 3)).reshape(n, N_HEADS * MLA_KV_LORA)

    w = N_HEADS * FOX_DH
    fk = z_fox[..., w:2 * w].reshape(b, t, N_HEADS, FOX_DH)
    fv = z_fox[..., 2 * w:3 * w].reshape(b, t, N_HEADS, FOX_DH)
    if past is None:
        q_f, k_f, v_f, lf = fox_prep(z_fox, lw['fox_bf'])
        o_c = flash_causal(q_f, k_f, v_f, FOX_DH)
        o_c = jnp.transpose(o_c[:, :, 0], (0, 2, 1, 3)).reshape(n, w)
    else:
        o_c, lf = fox_decode(z_fox, lw['fox_bf'], past['k_pool'], past['v_pool'], past['f_pool'],
                             past['page_table'], l * past['n_pool'])
        o_c = o_c.reshape(n, w)
    logf = lf[..., :N_HEADS]

    h = merge(x2, o_a.reshape(n, -1), o_lat, o_c, o_d.reshape(n, -1), z_gate, lw['mla_wuv'], lw['wb'], lw['wo'])
    state = {'mla_latent': c_lat, 'mla_rope': k_rope, 'fox_k': fk, 'fox_v': fv, 'fox_logf': logf,
             'gla': gla_s, 'gdn': gdn_s, 'gdn_conv': gdn_buf}
    return h, state


def _run_trunk(x, layer_weights, final_g, pos, past):
    b, t, d = x.shape
    states = []
    for l, lw in enumerate(layer_weights):
        h, st = _token_mix(x, lw, pos, past, l)
        xn, idx, gate = peer_select(h, lw['norm2_g'], lw['peer_wq'], lw['peer_sk'])
        x = peer_gather(idx, gate, xn, h, lw['peer_uv']).reshape(b, t, d)
        states.append(st)
    stacked = {name: jnp.stack([st[name] for st in states]) for name in states[0]}
    y = rmsnorm(x.reshape(b * t, d), final_g[None, :]).reshape(b, t, d)
    return y, stacked


def kernel(x_prompt, x_sample, cache_mla_latent, cache_mla_rope, cache_fox_k, cache_fox_v, cache_fox_logf,
           state_gla, state_gdn, state_gdn_conv, page_table,
           norm1_g, w_in, gla_w_a2, gla_b_a, gla_norm_g, mla_qnorm_g, mla_w_uq, mla_kvnorm_g, mla_w_uk, mla_w_uv,
           fox_b_f, gdn_conv_w, gdn_a_log, gdn_dt_bias, gdn_norm_g, w_branch, w_out, norm2_g,
           peer_w_q, peer_subkeys, peer_u, peer_v, final_norm_g):
    params = {
        'norm1_g': norm1_g, 'w_in': w_in, 'gla_w_a2': gla_w_a2, 'gla_b_a': gla_b_a, 'gla_norm_g': gla_norm_g,
        'mla_qnorm_g': mla_qnorm_g, 'mla_w_uq': mla_w_uq, 'mla_kvnorm_g': mla_kvnorm_g, 'mla_w_uk': mla_w_uk,
        'mla_w_uv': mla_w_uv, 'fox_b_f': fox_b_f, 'gdn_conv_w': gdn_conv_w, 'gdn_a_log': gdn_a_log,
        'gdn_dt_bias': gdn_dt_bias, 'gdn_norm_g': gdn_norm_g, 'w_branch': w_branch, 'w_out': w_out,
        'norm2_g': norm2_g, 'peer_w_q': peer_w_q, 'peer_subkeys': peer_subkeys, 'peer_u': peer_u, 'peer_v': peer_v,
    }
    depth, n_pool = cache_mla_latent.shape[:2]
    fox_w = N_HEADS * FOX_DH
    past = {
        'n_pool': n_pool,
        'lat_pool': cache_mla_latent.reshape(depth * n_pool, PAGE, MLA_KV_LORA),
        'rope_pool': cache_mla_rope.reshape(depth * n_pool, PAGE, MLA_ROPE),
        'k_pool': cache_fox_k.reshape(depth * n_pool, PAGE, fox_w),
        'v_pool': cache_fox_v.reshape(depth * n_pool, PAGE, fox_w),
        'f_pool': jnp.swapaxes(cache_fox_logf, 2, 3).reshape(depth * n_pool, N_HEADS, PAGE),
        'state_gla': state_gla, 'state_gdn': state_gdn, 'state_gdn_conv': state_gdn_conv, 'page_table': page_table,
    }
    layer_weights = [_layer_weights(l, params) for l in range(depth)]
    past_len = page_table.shape[1] * cache_mla_latent.shape[2]
    pos_prompt = jnp.arange(x_prompt.shape[1], dtype=I32)
    pos_sample = past_len + jnp.arange(x_sample.shape[1], dtype=I32)
    y_p, st_p = _run_trunk(x_prompt, layer_weights, final_norm_g, pos_prompt, None)
    y_s, st_s = _run_trunk(x_sample, layer_weights, final_norm_g, pos_sample, past)
    return (y_p, y_s,
            st_p['mla_latent'], st_s['mla_latent'], st_p['mla_rope'], st_s['mla_rope'],
            st_p['fox_k'], st_s['fox_k'], st_p['fox_v'], st_s['fox_v'],
            st_p['fox_logf'], st_s['fox_logf'], st_p['gla'], st_s['gla'],
            st_p['gdn'], st_s['gdn'], st_p['gdn_conv'], st_s['gdn_conv'])
```
